```python
import math
import jax, jax.numpy as jnp
from jax import lax
import numpy as np

D_MODEL = 1024
BATCH = 4
SEQ = 8192
DEPTH = 2
DEC_BATCH = 128
DEC_SEQ = 4
PAST_LEN = 16384
PAGE_SIZE = 128

N_A = DEPTH // 2
N_B = DEPTH - N_A
EXPAND = 2
D_INNER = EXPAND * D_MODEL
SSM_HEAD_DIM = 64
SSM_HEADS = D_INNER // SSM_HEAD_DIM
SSM_GROUPS = 4
HEADS_PER_GROUP = SSM_HEADS // SSM_GROUPS
D_STATE = 128
CONV_W = 4
CONV_DIM = D_INNER + 2 * SSM_GROUPS * D_STATE
D_IN_PROJ = D_INNER + CONV_DIM + SSM_HEADS
CHUNK = 128
WINDOW = 128
HEAD_DIM = 64
N_Q_HEADS = D_MODEL // HEAD_DIM
N_KV_HEADS = 4
Q_PER_KV = N_Q_HEADS // N_KV_HEADS
D_FF = 2816
EPS = 1e-6

kernel_name = 'yoco_ssd_swa_sink_decoder_step'


def rmsnorm(x, g):
    xf = x.astype(jnp.float32)
    y = xf * lax.rsqrt(jnp.mean(xf * xf, axis=-1, keepdims=True) + EPS)
    return (y * g.astype(jnp.float32)).astype(x.dtype)


def swiglu(h, w_gate, w_up, w_down):
    return (jax.nn.silu(h @ w_gate) * (h @ w_up)) @ w_down


def causal_dwconv(xbc, buf, w, b):
    xpad = jnp.concatenate([buf.astype(xbc.dtype), xbc], axis=1)
    out = lax.conv_general_dilated(xpad, w[:, None, :].astype(xbc.dtype), window_strides=(1,),
                                   padding='VALID', dimension_numbers=('NWC', 'WIO', 'NWC'),
                                   feature_group_count=CONV_DIM)
    return out + b.astype(xbc.dtype), xpad[:, xpad.shape[1] - (CONV_W - 1):]


def ssd_scan(xs, dt, a, bm, cm, h0):
    f32 = jnp.float32
    bsz, L = xs.shape[0], xs.shape[1]
    lc = min(CHUNK, L)
    pad = (-L) % lc
    def padt(t):
        return jnp.pad(t, [(0, 0), (0, pad)] + [(0, 0)] * (t.ndim - 2))
    xs, dt, bm, cm = (padt(t.astype(f32)) for t in (xs, dt, bm, cm))
    nc = (L + pad) // lc
    xg = (xs * dt[..., None]).reshape(bsz, nc, lc, SSM_GROUPS, HEADS_PER_GROUP, SSM_HEAD_DIM)
    ag = (dt * a).reshape(bsz, nc, lc, SSM_GROUPS, HEADS_PER_GROUP)
    bg = bm.reshape(bsz, nc, lc, SSM_GROUPS, D_STATE)
    cg = cm.reshape(bsz, nc, lc, SSM_GROUPS, D_STATE)
    seq = tuple(jnp.moveaxis(t, 1, 0) for t in (xg, ag, bg, cg))
    causal = jnp.tril(jnp.ones((lc, lc), dtype=bool))

    def step(h, inp):
        xk, ak, bk, ck = inp
        acum = jnp.cumsum(ak, axis=1)
        seg = acum[:, :, None] - acum[:, None, :]
        decay = jnp.exp(jnp.where(causal[None, :, :, None, None], seg, -jnp.inf))
        cb = jnp.einsum('btgn,bsgn->btsg', ck, bk)
        y = jnp.einsum('btsg,btsgh,bsghp->btghp', cb, decay, xk)
        y = y + jnp.einsum('btgn,bghpn->btghp', ck, h) * jnp.exp(acum)[..., None]
        last = acum[:, -1]
        w_in = jnp.exp(last[:, None] - acum)
        h = h * jnp.exp(last)[..., None, None] + jnp.einsum('bsgn,bsgh,bsghp->bghpn', bk, w_in, xk)
        return h, y

    h0g = h0.astype(f32).reshape(bsz, SSM_GROUPS, HEADS_PER_GROUP, SSM_HEAD_DIM, D_STATE)
    h, ys = lax.scan(step, h0g, seq)
    y = jnp.moveaxis(ys, 0, 1).reshape(bsz, nc * lc, SSM_HEADS, SSM_HEAD_DIM)[:, :L]
    return y, h.reshape(bsz, SSM_HEADS, SSM_HEAD_DIM, D_STATE)


def mamba_mixer(h, conv_buf, ssm_state, w_in, conv_w, conv_b, dt_bias, a_log, d_skip, norm_g, w_out):
    bsz, L, _ = h.shape
    f32 = jnp.float32
    zxbcdt = h @ w_in
    z, xbc, dt_raw = jnp.split(zxbcdt, [D_INNER, D_INNER + CONV_DIM], axis=-1)
    xbc, new_buf = causal_dwconv(xbc, conv_buf, conv_w, conv_b)
    xbc = jax.nn.silu(xbc)
    xs, bm, cm = jnp.split(xbc, [D_INNER, D_INNER + SSM_GROUPS * D_STATE], axis=-1)
    xs = xs.reshape(bsz, L, SSM_HEADS, SSM_HEAD_DIM)
    bm = bm.reshape(bsz, L, SSM_GROUPS, D_STATE)
    cm = cm.reshape(bsz, L, SSM_GROUPS, D_STATE)
    dt = jax.nn.softplus(dt_raw.astype(f32) + dt_bias.astype(f32))
    a = -jnp.exp(a_log.astype(f32))
    y, new_state = ssd_scan(xs, dt, a, bm, cm, ssm_state)
    y = y + d_skip.astype(f32)[:, None] * xs.astype(f32)
    y = y.reshape(bsz, L, D_INNER) * jax.nn.silu(z.astype(f32))
    yg = y.reshape(bsz, L, SSM_GROUPS, D_INNER // SSM_GROUPS)
    yg = yg * lax.rsqrt(jnp.mean(yg * yg, axis=-1, keepdims=True) + EPS)
    y = (yg.reshape(bsz, L, D_INNER) * norm_g.astype(f32)).astype(h.dtype)
    return y @ w_out, new_buf, new_state.astype(ssm_state.dtype)


def shared_kv(x, kv_norm, w_kv, b_kv):
    bsz, L, _ = x.shape
    kv = rmsnorm(x, kv_norm) @ w_kv + b_kv
    k, v = jnp.split(kv, 2, axis=-1)
    return (k.reshape(bsz, L, N_KV_HEADS, HEAD_DIM), v.reshape(bsz, L, N_KV_HEADS, HEAD_DIM))


def sink_attention(q, k, v, qpos, kpos, sinks):
    s = jnp.einsum('...qgrd,...kgd->...grqk', q, k).astype(jnp.float32) * (HEAD_DIM ** -0.5)
    rel = qpos[..., :, None] - kpos[..., None, :]
    mask = (rel >= 0) & (rel < WINDOW) & (kpos[..., None, :] >= 0)
    s = jnp.where(mask[..., None, None, :, :], s, -jnp.inf)
    sink = sinks.astype(jnp.float32).reshape(N_KV_HEADS, Q_PER_KV)[:, :, None, None]
    m = jnp.maximum(jnp.max(s, axis=-1, keepdims=True), sink)
    p = jnp.exp(s - m)
    denom = jnp.sum(p, axis=-1, keepdims=True) + jnp.exp(sink - m)
    return jnp.einsum('...grqk,...kgd->...qgrd', (p / denom).astype(v.dtype), v)


def window_attention_prompt(q, k, v, sinks):
    bsz, S = q.shape[0], q.shape[1]
    nb = S // WINDOW
    qb = q.reshape(bsz, nb, WINDOW, N_KV_HEADS, Q_PER_KV, HEAD_DIM)
    def band(t):
        tb = t.reshape(bsz, nb, WINDOW, N_KV_HEADS, HEAD_DIM)
        prev = jnp.pad(tb, ((0, 0), (1, 0), (0, 0), (0, 0), (0, 0)))[:, :-1]
        return jnp.concatenate([prev, tb], axis=2)
    pos = jnp.arange(S, dtype=jnp.int32).reshape(nb, WINDOW)
    kpos = jnp.concatenate([pos - WINDOW, pos], axis=1)
    o = sink_attention(qb, band(k), band(v), pos, kpos, sinks)
    return o.reshape(bsz, S, N_Q_HEADS * HEAD_DIM)


def window_attention_sample(q, kk, vv, sinks):
    bsz, T = q.shape[0], q.shape[1]
    qpos = PAST_LEN + jnp.arange(T, dtype=jnp.int32)
    kpos = PAST_LEN - WINDOW + jnp.arange(WINDOW + T, dtype=jnp.int32)
    o = sink_attention(q, kk, vv, qpos, kpos, sinks)
    return o.reshape(bsz, T, N_Q_HEADS * HEAD_DIM)


def trunk(x, conv_bufs, ssm_states, k_win, v_win, p):
    bsz, L, _ = x.shape
    is_prompt = k_win is None
    new_conv, new_ssm = [], []
    k = v = k_state = v_state = None
    for layer in range(DEPTH):
        g = p['norm_gain'][layer]
        x = x + 0.5 * swiglu(rmsnorm(x, g[0]), p['ffn_w_gate'][layer, 0], p['ffn_w_up'][layer, 0], p['ffn_w_down'][layer, 0])
        hn = rmsnorm(x, g[1])
        if layer < N_A:
            i = layer
            mix, cb, ss = mamba_mixer(hn, conv_bufs[i], ssm_states[i], p['ssm_w_in'][i], p['ssm_conv_w'][i],
                                      p['ssm_conv_b'][i], p['ssm_dt_bias'][i], p['ssm_a_log'][i],
                                      p['ssm_d'][i], p['ssm_norm'][i], p['ssm_w_out'][i])
            new_conv.append(cb)
            new_ssm.append(ss)
        else:
            j = layer - N_A
            q = (hn @ p['attn_w_q'][j] + p['attn_b_q'][j]).reshape(bsz, L, N_KV_HEADS, Q_PER_KV, HEAD_DIM)
            if is_prompt:
                o = window_attention_prompt(q, k, v, p['attn_sinks'][j])
            else:
                o = window_attention_sample(q, k, v, p['attn_sinks'][j])
            mix = o @ p['attn_w_o'][j] + p['attn_b_o'][j]
        x = x + mix
        x = x + 0.5 * swiglu(rmsnorm(x, g[2]), p['ffn_w_gate'][layer, 1], p['ffn_w_up'][layer, 1], p['ffn_w_down'][layer, 1])
        if layer == N_A - 1:
            k, v = shared_kv(x, p['kv_norm'], p['attn_w_kv'], p['attn_b_kv'])
            if not is_prompt:
                k = jnp.concatenate([k_win.astype(k.dtype), k], axis=1)
                v = jnp.concatenate([v_win.astype(v.dtype), v], axis=1)
            k_state = k[:, k.shape[1] - WINDOW:]
            v_state = v[:, v.shape[1] - WINDOW:]
    y = rmsnorm(x, p['final_norm'])
    return y, jnp.stack(new_conv), jnp.stack(new_ssm), k_state, v_state


def setup_inputs(seed: int = 0) -> dict:
    key = jax.random.key(seed)
    ks = iter(jax.random.split(key, 40))
    f32 = jnp.float32
    def nrm(shape, scale):
        return jax.random.normal(next(ks), shape, f32) * scale
    dt0 = jnp.exp(jax.random.uniform(next(ks), (N_A, SSM_HEADS), f32, math.log(1e-3), math.log(1e-1)))
    return {
        'x_prompt': nrm((BATCH, SEQ, D_MODEL), 1.0),
        'x_sample': nrm((DEC_BATCH, DEC_SEQ, D_MODEL), 1.0),
        'state_conv': nrm((N_A, DEC_BATCH, CONV_W - 1, CONV_DIM), 0.5),
        'state_ssm': nrm((N_A, DEC_BATCH, SSM_HEADS, SSM_HEAD_DIM, D_STATE), 0.1),
        'cache_k_win': nrm((DEC_BATCH, WINDOW, N_KV_HEADS, HEAD_DIM), 1.0),
        'cache_v_win': nrm((DEC_BATCH, WINDOW, N_KV_HEADS, HEAD_DIM), 1.0),
        'norm_gain': 1.0 + nrm((DEPTH, 3, D_MODEL), 0.02),
        'ffn_w_gate': nrm((DEPTH, 2, D_MODEL, D_FF), D_MODEL ** -0.5),
        'ffn_w_up': nrm((DEPTH, 2, D_MODEL, D_FF), D_MODEL ** -0.5),
        'ffn_w_down': nrm((DEPTH, 2, D_FF, D_MODEL), D_FF ** -0.5),
        'ssm_w_in': nrm((N_A, D_MODEL, D_IN_PROJ), D_MODEL ** -0.5),
        'ssm_conv_w': nrm((N_A, CONV_W, CONV_DIM), CONV_W ** -0.5),
        'ssm_conv_b': nrm((N_A, CONV_DIM), 0.02),
        'ssm_dt_bias': dt0 + jnp.log(-jnp.expm1(-dt0)),
        'ssm_a_log': jnp.log(jax.random.uniform(next(ks), (N_A, SSM_HEADS), f32, 1.0, 16.0)),
        'ssm_d': 1.0 + nrm((N_A, SSM_HEADS), 0.02),
        'ssm_norm': 1.0 + nrm((N_A, D_INNER), 0.02),
        'ssm_w_out': nrm((N_A, D_INNER, D_MODEL), D_INNER ** -0.5),
        'kv_norm': 1.0 + nrm((D_MODEL,), 0.02),
        'attn_w_kv': nrm((D_MODEL, 2 * N_KV_HEADS * HEAD_DIM), D_MODEL ** -0.5),
        'attn_b_kv': nrm((2 * N_KV_HEADS * HEAD_DIM,), 0.02),
        'attn_w_q': nrm((N_B, D_MODEL, N_Q_HEADS * HEAD_DIM), D_MODEL ** -0.5),
        'attn_b_q': nrm((N_B, N_Q_HEADS * HEAD_DIM), 0.02),
        'attn_sinks': nrm((N_B, N_Q_HEADS), 0.5),
        'attn_w_o': nrm((N_B, N_Q_HEADS * HEAD_DIM, D_MODEL), (N_Q_HEADS * HEAD_DIM) ** -0.5),
        'attn_b_o': nrm((N_B, D_MODEL), 0.02),
        'final_norm': 1.0 + nrm((D_MODEL,), 0.02),
    }


def reference(x_prompt, x_sample, state_conv, state_ssm, cache_k_win, cache_v_win, norm_gain,
              ffn_w_gate, ffn_w_up, ffn_w_down, ssm_w_in, ssm_conv_w, ssm_conv_b, ssm_dt_bias,
              ssm_a_log, ssm_d, ssm_norm, ssm_w_out, kv_norm, attn_w_kv, attn_b_kv, attn_w_q,
              attn_b_q, attn_sinks, attn_w_o, attn_b_o, final_norm):
    p = {'norm_gain': norm_gain, 'ffn_w_gate': ffn_w_gate, 'ffn_w_up': ffn_w_up, 'ffn_w_down': ffn_w_down,
         'ssm_w_in': ssm_w_in, 'ssm_conv_w': ssm_conv_w, 'ssm_conv_b': ssm_conv_b,
         'ssm_dt_bias': ssm_dt_bias, 'ssm_a_log': ssm_a_log, 'ssm_d': ssm_d, 'ssm_norm': ssm_norm,
         'ssm_w_out': ssm_w_out, 'kv_norm': kv_norm, 'attn_w_kv': attn_w_kv, 'attn_b_kv': attn_b_kv,
         'attn_w_q': attn_w_q, 'attn_b_q': attn_b_q, 'attn_sinks': attn_sinks, 'attn_w_o': attn_w_o,
         'attn_b_o': attn_b_o, 'final_norm': final_norm}
    bsz = x_prompt.shape[0]
    conv0 = jnp.zeros((N_A, bsz, CONV_W - 1, CONV_DIM), x_prompt.dtype)
    ssm0 = jnp.zeros((N_A, bsz, SSM_HEADS, SSM_HEAD_DIM, D_STATE), jnp.float32)
    y_prompt, conv_p, ssm_p, k_p, v_p = trunk(x_prompt, conv0, ssm0, None, None, p)
    y_sample, conv_s, ssm_s, k_s, v_s = trunk(x_sample, state_conv, state_ssm, cache_k_win, cache_v_win, p)
    return (y_prompt, y_sample, conv_p, ssm_p, k_p, v_p, conv_s, ssm_s, k_s, v_s)
```

```python
import functools

import jax
import jax.numpy as jnp
from jax import lax
from jax.experimental import pallas as pl
from jax.experimental.pallas import tpu as pltpu

F32 = jnp.float32
BF16 = jnp.bfloat16

D_MODEL = 1024
D_INNER = 2048
SSM_HEADS = 32
SSM_HEAD_DIM = 64
SSM_GROUPS = 4
D_STATE = 128
CONV_W = 4
CONV_DIM = D_INNER + 2 * SSM_GROUPS * D_STATE
GROUP_CH = D_INNER // SSM_GROUPS
CHUNK = 128
WINDOW = 128
HEAD_DIM = 64
N_Q_HEADS = 16
N_KV_HEADS = 4
Q_PER_KV = 4
KV_DIM = N_KV_HEADS * HEAD_DIM
D_FF = 2816
EPS = 1e-6

LANES = 128
FF_CHUNK = 256
N_FF_CHUNKS = D_FF // FF_CHUNK
TOKEN_TILE = 512
VMEM_LIMIT = 56 * 1024 * 1024


def _rms(x, g):
    ms = jnp.mean(x * x, axis=-1, keepdims=True)
    return x * lax.rsqrt(ms + EPS) * g


def _silu(x):
    return x * (1.0 / (1.0 + jnp.exp(-x)))


def _const_spec(shape):
    nd = len(shape)
    return pl.BlockSpec(shape, lambda *_: (0,) * nd, pipeline_mode=pl.Buffered(1))


def _params(n_axes):
    return pltpu.CompilerParams(dimension_semantics=("arbitrary",) * n_axes,
                                vmem_limit_bytes=VMEM_LIMIT)


def _ffn_kernel(x_ref, g_ref, wg_ref, wu_ref, wd_ref, fg_ref, o_ref, hn_ref, acc_ref, *, final):
    x = x_ref[...]
    hn_ref[...] = _rms(x, g_ref[...]).astype(BF16)
    acc_ref[...] = jnp.zeros_like(acc_ref)

    def body(c, carry):
        hn = hn_ref[...]
        gate = jnp.dot(hn, wg_ref[c], preferred_element_type=F32)
        up = jnp.dot(hn, wu_ref[c], preferred_element_type=F32)
        act = (_silu(gate) * up).astype(BF16)
        acc_ref[...] += jnp.dot(act, wd_ref[c], preferred_element_type=F32)
        return carry

    lax.fori_loop(0, N_FF_CHUNKS, body, 0)
    out = x + 0.5 * acc_ref[...]
    if final:
        out = _rms(out, fg_ref[...])
    o_ref[...] = out


def _ffn(x, g, wg, wu, wd, final_g=None):
    m = x.shape[0]
    tm = min(TOKEN_TILE, m)
    final = final_g is not None
    fg = final_g if final else g
    return pl.pallas_call(
        functools.partial(_ffn_kernel, final=final),
        grid=(m // tm,),
        in_specs=[
            pl.BlockSpec((tm, D_MODEL), lambda i: (i, 0)),
            _const_spec((1, D_MODEL)),
            _const_spec(wg.shape),
            _const_spec(wu.shape),
            _const_spec(wd.shape),
            _const_spec((1, D_MODEL)),
        ],
        out_specs=pl.BlockSpec((tm, D_MODEL), lambda i: (i, 0)),
        out_shape=jax.ShapeDtypeStruct((m, D_MODEL), F32),
        scratch_shapes=[pltpu.VMEM((tm, D_MODEL), BF16), pltpu.VMEM((tm, D_MODEL), F32)],
        compiler_params=_params(1),
        name="ffn_final" if final else "ffn",
    )(x, g.reshape(1, D_MODEL), wg, wu, wd, fg.reshape(1, D_MODEL))


PROJ_CHUNK = 512


def _proj_kernel(*refs, n_out):
    x_ref, g_ref = refs[0], refs[1]
    w_refs = refs[2:2 + n_out]
    b_refs = refs[2 + n_out:2 + 2 * n_out]
    o_refs = refs[2 + 2 * n_out:2 + 3 * n_out]
    hn_ref = refs[2 + 3 * n_out]
    hn_ref[...] = _rms(x_ref[...], g_ref[...]).astype(BF16)
    for w_ref, b_ref, o_ref in zip(w_refs, b_refs, o_refs):
        n = w_ref.shape[1]
        for c0 in range(0, n, PROJ_CHUNK):
            c1 = min(c0 + PROJ_CHUNK, n)
            r = jnp.dot(hn_ref[...], w_ref[:, c0:c1], preferred_element_type=F32)
            o_ref[:, c0:c1] = (r + b_ref[:, c0:c1]).astype(o_ref.dtype)


def _proj(x, g, ws, bs, dtypes, name):
    m = x.shape[0]
    tm = min(TOKEN_TILE, m)
    n_out = len(ws)
    bs = [b.reshape(1, -1).astype(F32) for b in bs]
    return pl.pallas_call(
        functools.partial(_proj_kernel, n_out=n_out),
        grid=(m // tm,),
        in_specs=[pl.BlockSpec((tm, D_MODEL), lambda i: (i, 0)), _const_spec((1, D_MODEL))]
        + [_const_spec(w.shape) for w in ws] + [_const_spec(b.shape) for b in bs],
        out_specs=[pl.BlockSpec((tm, w.shape[1]), lambda i: (i, 0)) for w in ws],
        out_shape=[jax.ShapeDtypeStruct((m, w.shape[1]), dt) for w, dt in zip(ws, dtypes)],
        scratch_shapes=[pltpu.VMEM((tm, D_MODEL), BF16)],
        compiler_params=_params(1),
        name=name,
    )(x, g.reshape(1, D_MODEL), *ws, *bs)


def _matres_kernel(a_ref, w_ref, b_ref, res_ref, o_ref):
    for c0 in range(0, D_MODEL, PROJ_CHUNK):
        c1 = c0 + PROJ_CHUNK
        r = jnp.dot(a_ref[...], w_ref[:, c0:c1], preferred_element_type=F32)
        o_ref[:, c0:c1] = res_ref[:, c0:c1] + r + b_ref[:, c0:c1]


def _matres(a, w, b, res, name):
    m, k = a.shape
    tm = min(TOKEN_TILE, m)
    return pl.pallas_call(
        _matres_kernel,
        grid=(m // tm,),
        in_specs=[pl.BlockSpec((tm, k), lambda i: (i, 0)), _const_spec(w.shape),
                  _const_spec((1, D_MODEL)), pl.BlockSpec((tm, D_MODEL), lambda i: (i, 0))],
        out_specs=pl.BlockSpec((tm, D_MODEL), lambda i: (i, 0)),
        out_shape=jax.ShapeDtypeStruct((m, D_MODEL), F32),
        compiler_params=_params(1),
        name=name,
    )(a, w, b.reshape(1, D_MODEL).astype(F32), res)


CONV_PAD = 8


def _split3(v):
    hi = v.astype(BF16)
    r1 = v - hi.astype(F32)
    mid = r1.astype(BF16)
    lo = (r1 - mid.astype(F32)).astype(BF16)
    return hi, mid, lo


def _ssd_kernel(z_ref, xbc_ref, dt_ref, conv0_ref, ssm0_ref, cw_ref, cb_ref, dtb_ref, alog_ref,
                dsk_ref, ng_ref, y_ref, convt_ref, ssmt_ref,
                xpad_ref, xs_ref, bc_ref, st_ref, dtpad_ref, zpad_ref, *, lin, lvalid, nchunks):
    c = pl.program_id(1)
    L = CHUNK
    n_pairs = SSM_HEADS // 2

    @pl.when(c == 0)
    def _init():
        xpad_ref[...] = jnp.zeros_like(xpad_ref)
        dtpad_ref[...] = jnp.zeros_like(dtpad_ref)
        zpad_ref[...] = jnp.zeros_like(zpad_ref)
        xpad_ref[CONV_PAD - 3:CONV_PAD, :] = conv0_ref[0]
        for j in range(n_pairs):
            blk = ssm0_ref[0, 2 * j:2 * j + 2].reshape(LANES, D_STATE)
            st_ref[:, LANES * j:LANES * (j + 1)] = blk.T

    xpad_ref[CONV_PAD:CONV_PAD + lin, :] = xbc_ref[0].astype(F32)
    slab = GROUP_CH
    for s in range(CONV_DIM // slab):
        sl = slice(slab * s, slab * (s + 1))
        acc = cb_ref[:, sl] + cw_ref[0:1, sl] * xpad_ref[CONV_PAD - 3:CONV_PAD - 3 + L, sl]
        acc = acc + cw_ref[1:2, sl] * xpad_ref[CONV_PAD - 2:CONV_PAD - 2 + L, sl]
        acc = acc + cw_ref[2:3, sl] * xpad_ref[CONV_PAD - 1:CONV_PAD - 1 + L, sl]
        acc = acc + cw_ref[3:4, sl] * xpad_ref[CONV_PAD:CONV_PAD + L, sl]
        act = _silu(acc).astype(BF16)
        if s < D_INNER // slab:
            xs_ref[:, sl] = act
        else:
            o = slab * s - D_INNER
            bc_ref[:, o:o + slab] = act
    tail = xpad_ref[CONV_PAD + lvalid - 3:CONV_PAD + lvalid, :]
    xpad_ref[CONV_PAD - 3:CONV_PAD, :] = tail

    @pl.when(c == nchunks - 1)
    def _conv_out():
        convt_ref[0] = tail

    if lin < L:
        dtpad_ref[0:lin, :] = dt_ref[0]
        zpad_ref[0:lin, :] = z_ref[0]
        dt_raw = dtpad_ref[...]
        z_src = zpad_ref
    else:
        dt_raw = dt_ref[0]
        z_src = z_ref.at[0]
    row_i = lax.broadcasted_iota(jnp.int32, (L, L), 0)
    col_i = lax.broadcasted_iota(jnp.int32, (L, L), 1)
    causal = col_i <= row_i
    lane_lo = col_i < SSM_HEAD_DIM
    dtv = jax.nn.softplus(dt_raw + dtb_ref[...])
    if lvalid < L:
        dtv = jnp.where(row_i < lvalid, dtv, 0.0)
    a = dtv * (-jnp.exp(alog_ref[...]))
    tri = jnp.where(causal, 1.0, 0.0).astype(BF16)
    acum_col = sum(jnp.dot(tri, part, preferred_element_type=F32) for part in _split3(a))
    acum_row = acum_col.T
    dt_row = dtv.T
    last = jnp.broadcast_to(acum_row[:, L - 1:L], (L, L))
    wd_row = jnp.exp(last - acum_row) * dt_row

    for g in range(SSM_GROUPS):
        bg = bc_ref[:, D_STATE * g:D_STATE * (g + 1)]
        cg = bc_ref[:, GROUP_CH + D_STATE * g:GROUP_CH + D_STATE * (g + 1)]
        cb = lax.dot_general(cg, bg, (((1,), (1,)), ((), ())), preferred_element_type=F32)
        bgt = bg.astype(F32).T
        gsl = slice(GROUP_CH * g, GROUP_CH * (g + 1))
        y_inter = jnp.dot(cg, st_ref[:, gsl].astype(BF16), preferred_element_type=F32)
        outs = []
        for j in range(GROUP_CH // LANES):
            h0 = (GROUP_CH // SSM_HEAD_DIM) * g + 2 * j
            psl = slice(GROUP_CH * g + LANES * j, GROUP_CH * g + LANES * (j + 1))
            acs, lms, wms = [], [], []
            for h in (h0, h0 + 1):
                ac = jnp.broadcast_to(acum_col[:, h:h + 1], (L, L))
                seg = ac - acum_row[h:h + 1, :]
                lm = jnp.where(causal, jnp.exp(seg), 0.0) * cb * dt_row[h:h + 1, :]
                acs.append(ac)
                lms.append(lm.astype(BF16))
                wms.append((bgt * wd_row[h:h + 1, :]).astype(BF16))
            x = xs_ref[:, psl]
            zero = jnp.zeros_like(x)
            xbd = jnp.concatenate([jnp.where(lane_lo, x, zero), jnp.where(lane_lo, zero, x)], axis=0)
            y_intra = jnp.dot(jnp.concatenate(lms, axis=1), xbd, preferred_element_type=F32)
            acp = jnp.where(lane_lo, acs[0], acs[1])
            y = y_intra + y_inter[:, LANES * j:LANES * (j + 1)] * jnp.exp(acp)
            outs.append(y + dsk_ref[:, psl] * x.astype(F32))
            upd = jnp.dot(jnp.concatenate(wms, axis=1), xbd, preferred_element_type=F32)
            st_ref[:, psl] = st_ref[:, psl] * jnp.exp(acp[L - 1:L, :]) + upd
        yg = jnp.concatenate(outs, axis=1)
        zg = z_src[:, gsl].astype(F32)
        gated = yg * _silu(zg)
        ms = jnp.mean(gated * gated, axis=-1, keepdims=True)
        yn = gated * lax.rsqrt(ms + EPS) * ng_ref[:, gsl]
        y_ref[0, :, gsl] = yn[0:lin].astype(y_ref.dtype)

    @pl.when(c == nchunks - 1)
    def _state_out():
        for j in range(n_pairs):
            blk = st_ref[:, LANES * j:LANES * (j + 1)].T
            ssmt_ref[0, 2 * j:2 * j + 2] = blk.reshape(2, SSM_HEAD_DIM, D_STATE)


def _ssd(z, xbc, dt, conv0, ssm0, cw, cb, dtb, alog, dsk, ng, *, lin, lvalid):
    nseq, ltot, _ = z.shape
    nchunks = ltot // lin
    seq_blk = lambda last: pl.BlockSpec((1, lin, last), lambda s, c: (s, c, 0))
    return pl.pallas_call(
        functools.partial(_ssd_kernel, lin=lin, lvalid=lvalid, nchunks=nchunks),
        grid=(nseq, nchunks),
        in_specs=[
            seq_blk(D_INNER), seq_blk(CONV_DIM), seq_blk(LANES),
            pl.BlockSpec((1, CONV_W - 1, CONV_DIM), lambda s, c: (s, 0, 0)),
            pl.BlockSpec((1, SSM_HEADS, SSM_HEAD_DIM, D_STATE), lambda s, c: (s, 0, 0, 0)),
            _const_spec((CONV_W, CONV_DIM)), _const_spec((1, CONV_DIM)),
            _const_spec((1, LANES)), _const_spec((1, LANES)),
            _const_spec((1, D_INNER)), _const_spec((1, D_INNER)),
        ],
        out_specs=[
            seq_blk(D_INNER),
            pl.BlockSpec((1, CONV_W - 1, CONV_DIM), lambda s, c: (s, 0, 0)),
            pl.BlockSpec((1, SSM_HEADS, SSM_HEAD_DIM, D_STATE), lambda s, c: (s, 0, 0, 0)),
        ],
        out_shape=[
            jax.ShapeDtypeStruct((nseq, ltot, D_INNER), BF16),
            jax.ShapeDtypeStruct((nseq, CONV_W - 1, CONV_DIM), F32),
            jax.ShapeDtypeStruct((nseq, SSM_HEADS, SSM_HEAD_DIM, D_STATE), F32),
        ],
        scratch_shapes=[
            pltpu.VMEM((CONV_PAD + CHUNK, CONV_DIM), F32),
            pltpu.VMEM((CHUNK, D_INNER), BF16),
            pltpu.VMEM((CHUNK, 2 * GROUP_CH), BF16),
            pltpu.VMEM((D_STATE, D_INNER), F32),
            pltpu.VMEM((CHUNK, LANES), F32),
            pltpu.VMEM((CHUNK, D_INNER), BF16),
        ],
        compiler_params=_params(2),
        name="ssd",
    )(z, xbc, dt, conv0, ssm0, cw, cb, dtb, alog, dsk, ng)


def _attn_kernel(q_ref, kp_ref, vp_ref, kc_ref, vc_ref, sink_ref, o_ref, *, tq, first_has_prev):
    i = pl.program_id(1)
    q = q_ref[0] * (HEAD_DIM ** -0.5)
    kp = kp_ref[0].astype(BF16)
    vp = vp_ref[0].astype(BF16)
    kc = kc_ref[0].astype(BF16)
    vc = vc_ref[0].astype(BF16)
    t_p = lax.broadcasted_iota(jnp.int32, (tq, WINDOW), 0)
    j_p = lax.broadcasted_iota(jnp.int32, (tq, WINDOW), 1)
    if first_has_prev:
        mask_prev = j_p > t_p
    else:
        mask_prev = j_p > t_p + jnp.where(i > 0, 0, WINDOW)
    t_c = lax.broadcasted_iota(jnp.int32, (tq, tq), 0)
    j_c = lax.broadcasted_iota(jnp.int32, (tq, tq), 1)
    mask_cur = j_c <= t_c
    nt = (((1,), (1,)), ((), ()))
    outs = []
    for h in range(N_Q_HEADS):
        g = h // Q_PER_KV
        hs = slice(HEAD_DIM * h, HEAD_DIM * (h + 1))
        gs = slice(HEAD_DIM * g, HEAD_DIM * (g + 1))
        qh = q[:, hs]
        s_p = lax.dot_general(qh, kp[:, gs], nt, preferred_element_type=F32)
        s_c = lax.dot_general(qh, kc[:, gs], nt, preferred_element_type=F32)
        s_p = jnp.where(mask_prev, s_p, -jnp.inf)
        s_c = jnp.where(mask_cur, s_c, -jnp.inf)
        sink = sink_ref[:, h:h + 1]
        m = jnp.maximum(jnp.maximum(jnp.max(s_p, axis=-1, keepdims=True),
                                    jnp.max(s_c, axis=-1, keepdims=True)), sink)
        p_p = jnp.exp(s_p - m)
        p_c = jnp.exp(s_c - m)
        denom = (jnp.sum(p_p, axis=-1, keepdims=True) + jnp.sum(p_c, axis=-1, keepdims=True)
                 + jnp.exp(sink - m))
        pv = (jnp.dot(p_p.astype(BF16), vp[:, gs], preferred_element_type=F32)
              + jnp.dot(p_c.astype(BF16), vc[:, gs], preferred_element_type=F32))
        outs.append(pv / denom)
    o_ref[0] = jnp.concatenate(outs, axis=1).astype(o_ref.dtype)


def _attn(q, k_prev, v_prev, k_cur, v_cur, sinks, *, tq, first_has_prev):
    nseq, ltot, _ = q.shape
    nblk = ltot // tq
    if first_has_prev:
        prev_map = lambda s, i: (s, 0, 0)
    else:
        prev_map = lambda s, i: (s, jnp.maximum(i - 1, 0), 0)
    cur_map = lambda s, i: (s, i, 0)
    return pl.pallas_call(
        functools.partial(_attn_kernel, tq=tq, first_has_prev=first_has_prev),
        grid=(nseq, nblk),
        in_specs=[
            pl.BlockSpec((1, tq, D_MODEL), cur_map),
            pl.BlockSpec((1, WINDOW, KV_DIM), prev_map),
            pl.BlockSpec((1, WINDOW, KV_DIM), prev_map),
            pl.BlockSpec((1, tq, KV_DIM), cur_map),
            pl.BlockSpec((1, tq, KV_DIM), cur_map),
            _const_spec((1, LANES)),
        ],
        out_specs=pl.BlockSpec((1, tq, D_MODEL), cur_map),
        out_shape=jax.ShapeDtypeStruct((nseq, ltot, D_MODEL), BF16),
        compiler_params=_params(2),
        name="attn",
    )(q, k_prev, v_prev, k_cur, v_cur, sinks)


def _trunk(x3, conv0, ssm0, k_win, v_win, p, *, lin, lvalid):
    nseq, ltot, _ = x3.shape
    m = nseq * ltot
    x = x3.reshape(m, D_MODEL)
    g = p["norm_gain"]
    x = _ffn(x, g[0, 0], *p["ffn"][0][0])
    z, xbc, dt = _proj(x, g[0, 1], [p["w_z"], p["w_xbc"], p["w_dt"]],
                       [jnp.zeros((D_INNER,), F32), jnp.zeros((CONV_DIM,), F32), jnp.zeros((LANES,), F32)],
                       [BF16, BF16, F32], "in_proj")
    y, conv_t, ssm_t = _ssd(z.reshape(nseq, ltot, D_INNER), xbc.reshape(nseq, ltot, CONV_DIM),
                            dt.reshape(nseq, ltot, LANES), conv0, ssm0, p["conv_w"], p["conv_b"],
                            p["dt_bias"], p["a_log"], p["d_skip"], p["ssm_norm"], lin=lin, lvalid=lvalid)
    x = _matres(y.reshape(m, D_INNER), p["w_out"], jnp.zeros((D_MODEL,), F32), x, "out_proj")
    x = _ffn(x, g[0, 2], *p["ffn"][0][1])
    k, v = _proj(x, p["kv_norm"], [p["w_k"], p["w_v"]], [p["b_k"], p["b_v"]], [F32, F32], "kv_proj")
    x = _ffn(x, g[1, 0], *p["ffn"][1][0])
    (q,) = _proj(x, g[1, 1], [p["w_q"]], [p["b_q"]], [BF16], "q_proj")
    q3 = q.reshape(nseq, ltot, D_MODEL)
    k3 = k.reshape(nseq, ltot, KV_DIM)
    v3 = v.reshape(nseq, ltot, KV_DIM)
    if k_win is None:
        o = _attn(q3, k3, v3, k3, v3, p["sinks"], tq=lin, first_has_prev=False)
    else:
        o = _attn(q3, k_win, v_win, k3, v3, p["sinks"], tq=lin, first_has_prev=True)
    x = _matres(o.reshape(m, D_MODEL), p["w_o"], p["b_o"], x, "o_proj")
    y_out = _ffn(x, g[1, 2], *p["ffn"][1][1], final_g=p["final_norm"])
    return y_out.reshape(nseq, ltot, D_MODEL), conv_t, ssm_t, k3, v3


def kernel(x_prompt, x_sample, state_conv, state_ssm, cache_k_win, cache_v_win, norm_gain, ffn_w_gate, ffn_w_up, ffn_w_down, ssm_w_in, ssm_conv_w, ssm_conv_b, ssm_dt_bias, ssm_a_log, ssm_d, ssm_norm, ssm_w_out, kv_norm, attn_w_kv, attn_b_kv, attn_w_q, attn_b_q, attn_sinks, attn_w_o, attn_b_o, final_norm):
    depth = ffn_w_gate.shape[0]
    assert depth == 2 and ssm_w_in.shape[0] == 1 and attn_w_q.shape[0] == 1

    def chunked_cols(w):
        return w.astype(BF16).reshape(D_MODEL, N_FF_CHUNKS, FF_CHUNK).transpose(1, 0, 2)

    ffn = [[(chunked_cols(ffn_w_gate[l, i]), chunked_cols(ffn_w_up[l, i]),
             ffn_w_down[l, i].astype(BF16).reshape(N_FF_CHUNKS, FF_CHUNK, D_MODEL))
            for i in range(2)] for l in range(depth)]
    w_in = ssm_w_in[0].astype(BF16)
    pad_heads = lambda v: jnp.pad(v.astype(F32), (0, LANES - SSM_HEADS)).reshape(1, LANES)
    w_kv = attn_w_kv.astype(BF16)
    p = {
        "norm_gain": norm_gain,
        "ffn": ffn,
        "w_z": w_in[:, :D_INNER],
        "w_xbc": w_in[:, D_INNER:D_INNER + CONV_DIM],
        "w_dt": jnp.pad(w_in[:, D_INNER + CONV_DIM:], ((0, 0), (0, LANES - SSM_HEADS))),
        "conv_w": ssm_conv_w[0].astype(F32),
        "conv_b": ssm_conv_b[0].astype(F32).reshape(1, CONV_DIM),
        "dt_bias": pad_heads(ssm_dt_bias[0]),
        "a_log": pad_heads(ssm_a_log[0]),
        "d_skip": jnp.repeat(ssm_d[0].astype(F32), SSM_HEAD_DIM).reshape(1, D_INNER),
        "ssm_norm": ssm_norm[0].astype(F32).reshape(1, D_INNER),
        "w_out": ssm_w_out[0].astype(BF16),
        "kv_norm": kv_norm,
        "w_k": w_kv[:, :KV_DIM], "w_v": w_kv[:, KV_DIM:],
        "b_k": attn_b_kv[:KV_DIM], "b_v": attn_b_kv[KV_DIM:],
        "w_q": attn_w_q[0].astype(BF16),
        "b_q": attn_b_q[0],
        "sinks": jnp.pad(attn_sinks[0].astype(F32), (0, LANES - N_Q_HEADS)).reshape(1, LANES),
        "w_o": attn_w_o[0].astype(BF16),
        "b_o": attn_b_o[0],
        "final_norm": final_norm,
    }

    bsz, seq, _ = x_prompt.shape
    conv0 = jnp.zeros((bsz, CONV_W - 1, CONV_DIM), F32)
    ssm0 = jnp.zeros((bsz, SSM_HEADS, SSM_HEAD_DIM, D_STATE), F32)
    y_p, conv_p, ssm_p, k_p, v_p = _trunk(x_prompt, conv0, ssm0, None, None, p, lin=CHUNK, lvalid=CHUNK)
    k_win_p = k_p[:, seq - WINDOW:].reshape(bsz, WINDOW, N_KV_HEADS, HEAD_DIM)
    v_win_p = v_p[:, seq - WINDOW:].reshape(bsz, WINDOW, N_KV_HEADS, HEAD_DIM)

    dbsz, dseq, _ = x_sample.shape
    assert CONV_W - 1 <= dseq <= 8
    lin_s = 8
    xs = jnp.pad(x_sample, ((0, 0), (0, lin_s - dseq), (0, 0)))
    ck = cache_k_win.reshape(dbsz, WINDOW, KV_DIM)
    cv = cache_v_win.reshape(dbsz, WINDOW, KV_DIM)
    y_s, conv_s, ssm_s, k_s, v_s = _trunk(xs, state_conv[0], state_ssm[0], ck, cv, p, lin=lin_s, lvalid=dseq)
    k_win_s = jnp.concatenate([ck[:, dseq:], k_s[:, :dseq]], axis=1).reshape(dbsz, WINDOW, N_KV_HEADS, HEAD_DIM)
    v_win_s = jnp.concatenate([cv[:, dseq:], v_s[:, :dseq]], axis=1).reshape(dbsz, WINDOW, N_KV_HEADS, HEAD_DIM)
    return (y_p, y_s[:, :dseq], conv_p[None], ssm_p[None], k_win_p, v_win_p,
            conv_s[None], ssm_s[None], k_win_s, v_win_s)
```

```python
import functools

import jax
import jax.numpy as jnp
from jax import lax
from jax.experimental import pallas as pl
from jax.experimental.pallas import tpu as pltpu

F32 = jnp.float32
BF16 = jnp.bfloat16

D_MODEL = 1024
D_INNER = 2048
SSM_HEADS = 32
SSM_HEAD_DIM = 64
SSM_GROUPS = 4
D_STATE = 128
CONV_W = 4
CONV_DIM = D_INNER + 2 * SSM_GROUPS * D_STATE
GROUP_CH = D_INNER // SSM_GROUPS
CHUNK = 128
WINDOW = 128
HEAD_DIM = 64
N_Q_HEADS = 16
N_KV_HEADS = 4
Q_PER_KV = 4
KV_DIM = N_KV_HEADS * HEAD_DIM
D_FF = 2816
EPS = 1e-6

LANES = 128
FF_CHUNK = 256
N_FF_CHUNKS = D_FF // FF_CHUNK
TOKEN_TILE = 512
VMEM_LIMIT = 56 * 1024 * 1024


def _rms(x, g):
    ms = jnp.mean(x * x, axis=-1, keepdims=True)
    return x * lax.rsqrt(ms + EPS) * g


def _silu(x):
    hx = 0.5 * x
    return hx + hx * jnp.tanh(hx)


def _const_spec(shape):
    nd = len(shape)
    return pl.BlockSpec(shape, lambda *_: (0,) * nd, pipeline_mode=pl.Buffered(1))


def _params(n_axes):
    return pltpu.CompilerParams(dimension_semantics=("arbitrary",) * n_axes,
                                vmem_limit_bytes=VMEM_LIMIT)


def _ffn_kernel(x_ref, g_ref, wg_ref, wu_ref, wd_ref, fg_ref, o_ref, hn_ref, acc_ref, *, final):
    x = x_ref[...]
    hn_ref[...] = _rms(x, g_ref[...]).astype(BF16)
    acc_ref[...] = jnp.zeros_like(acc_ref)

    def body(c, carry):
        hn = hn_ref[...]
        gate = jnp.dot(hn, wg_ref[c], preferred_element_type=F32)
        up = jnp.dot(hn, wu_ref[c], preferred_element_type=F32)
        act = (_silu(gate) * up).astype(BF16)
        acc_ref[...] += jnp.dot(act, wd_ref[c], preferred_element_type=F32)
        return carry

    lax.fori_loop(0, N_FF_CHUNKS, body, 0, unroll=True)
    out = x + 0.5 * acc_ref[...]
    if final:
        out = _rms(out, fg_ref[...])
    o_ref[...] = out


def _ffn(x, g, wg, wu, wd, final_g=None):
    m = x.shape[0]
    tm = min(TOKEN_TILE, m)
    final = final_g is not None
    fg = final_g if final else g
    return pl.pallas_call(
        functools.partial(_ffn_kernel, final=final),
        grid=(m // tm,),
        in_specs=[
            pl.BlockSpec((tm, D_MODEL), lambda i: (i, 0)),
            _const_spec((1, D_MODEL)),
            _const_spec(wg.shape),
            _const_spec(wu.shape),
            _const_spec(wd.shape),
            _const_spec((1, D_MODEL)),
        ],
        out_specs=pl.BlockSpec((tm, D_MODEL), lambda i: (i, 0)),
        out_shape=jax.ShapeDtypeStruct((m, D_MODEL), F32),
        scratch_shapes=[pltpu.VMEM((tm, D_MODEL), BF16), pltpu.VMEM((tm, D_MODEL), F32)],
        compiler_params=_params(1),
        name="ffn_final" if final else "ffn",
    )(x, g.reshape(1, D_MODEL), wg, wu, wd, fg.reshape(1, D_MODEL))


PROJ_CHUNK = 512
PROJ_T_CHUNK = 256


def _proj_kernel(*refs, n_out, transposed):
    x_ref, g_ref = refs[0], refs[1]
    w_refs = refs[2:2 + n_out]
    b_refs = refs[2 + n_out:2 + 2 * n_out]
    o_refs = refs[2 + 2 * n_out:2 + 3 * n_out]
    hn_ref = refs[2 + 3 * n_out]
    hn_ref[...] = _rms(x_ref[...], g_ref[...]).astype(BF16)
    tm = x_ref.shape[0]
    for w_ref, b_ref, o_ref, tr in zip(w_refs, b_refs, o_refs, transposed):
        if tr:
            for c0 in range(0, tm, PROJ_T_CHUNK):
                r = lax.dot_general(w_ref[...], hn_ref[c0:c0 + PROJ_T_CHUNK, :], (((1,), (1,)), ((), ())),
                                    preferred_element_type=F32)
                o_ref[:, c0:c0 + PROJ_T_CHUNK] = (r + b_ref[...]).astype(o_ref.dtype)
        else:
            n = w_ref.shape[1]
            for c0 in range(0, n, PROJ_CHUNK):
                c1 = min(c0 + PROJ_CHUNK, n)
                r = jnp.dot(hn_ref[...], w_ref[:, c0:c1], preferred_element_type=F32)
                o_ref[:, c0:c1] = (r + b_ref[:, c0:c1]).astype(o_ref.dtype)


def _proj(x, g, ws, bs, dtypes, name, transposed=None):
    m = x.shape[0]
    tm = min(TOKEN_TILE, m)
    n_out = len(ws)
    transposed = tuple(transposed or (False,) * n_out)
    bs = [jnp.broadcast_to(b.astype(F32).reshape(-1, 1), (b.size, PROJ_T_CHUNK)) if tr
          else b.reshape(1, -1).astype(F32) for b, tr in zip(bs, transposed)]
    out_specs, out_shape = [], []
    for w, dt, tr in zip(ws, dtypes, transposed):
        if tr:
            out_specs.append(pl.BlockSpec((w.shape[0], tm), lambda i: (0, i)))
            out_shape.append(jax.ShapeDtypeStruct((w.shape[0], m), dt))
        else:
            out_specs.append(pl.BlockSpec((tm, w.shape[1]), lambda i: (i, 0)))
            out_shape.append(jax.ShapeDtypeStruct((m, w.shape[1]), dt))
    return pl.pallas_call(
        functools.partial(_proj_kernel, n_out=n_out, transposed=transposed),
        grid=(m // tm,),
        in_specs=[pl.BlockSpec((tm, D_MODEL), lambda i: (i, 0)), _const_spec((1, D_MODEL))]
        + [_const_spec(w.shape) for w in ws] + [_const_spec(b.shape) for b in bs],
        out_specs=out_specs,
        out_shape=out_shape,
        scratch_shapes=[pltpu.VMEM((tm, D_MODEL), BF16)],
        compiler_params=_params(1),
        name=name,
    )(x, g.reshape(1, D_MODEL), *ws, *bs)


def _matres_kernel(a_ref, w_ref, b_ref, res_ref, o_ref):
    for c0 in range(0, D_MODEL, PROJ_CHUNK):
        c1 = c0 + PROJ_CHUNK
        r = jnp.dot(a_ref[...], w_ref[:, c0:c1], preferred_element_type=F32)
        o_ref[:, c0:c1] = res_ref[:, c0:c1] + r + b_ref[:, c0:c1]


def _matres(a, w, b, res, name):
    m, k = a.shape
    tm = min(TOKEN_TILE, m)
    return pl.pallas_call(
        _matres_kernel,
        grid=(m // tm,),
        in_specs=[pl.BlockSpec((tm, k), lambda i: (i, 0)), _const_spec(w.shape),
                  _const_spec((1, D_MODEL)), pl.BlockSpec((tm, D_MODEL), lambda i: (i, 0))],
        out_specs=pl.BlockSpec((tm, D_MODEL), lambda i: (i, 0)),
        out_shape=jax.ShapeDtypeStruct((m, D_MODEL), F32),
        compiler_params=_params(1),
        name=name,
    )(a, w, b.reshape(1, D_MODEL).astype(F32), res)


CONV_PAD = 8


def _split3(v):
    hi = v.astype(BF16)
    r1 = v - hi.astype(F32)
    mid = r1.astype(BF16)
    lo = (r1 - mid.astype(F32)).astype(BF16)
    return hi, mid, lo


def _ssd_kernel(z_ref, xbc_ref, dt_ref, conv0_ref, ssm0_ref, cw_ref, cb_ref, dtb_ref, alog_ref,
                dsk_ref, ng_ref, y_ref, convt_ref, ssmt_ref,
                xpad_ref, xs_ref, bc_ref, st_ref, dtpad_ref, zpad_ref, xcur_ref, *, lin, lvalid, nchunks):
    c = pl.program_id(1)
    L = CHUNK
    n_pairs = SSM_HEADS // 2
    assert CONV_W - 1 <= lvalid <= lin and (lvalid - 1) // 16 == (lvalid - 3) // 16

    @pl.when(c == 0)
    def _init():
        xpad_ref[...] = jnp.zeros_like(xpad_ref)
        xcur_ref[...] = jnp.zeros_like(xcur_ref)
        dtpad_ref[...] = jnp.zeros_like(dtpad_ref)
        zpad_ref[...] = jnp.zeros_like(zpad_ref)
        xpad_ref[CONV_PAD - 3:CONV_PAD, :] = conv0_ref[0]
        for j in range(n_pairs):
            blk = ssm0_ref[0, 2 * j:2 * j + 2].reshape(LANES, D_STATE)
            st_ref[:, LANES * j:LANES * (j + 1)] = blk.T

    if lin < L:
        xcur_ref[0:lin, :] = xbc_ref[0]
        x_src = xcur_ref
    else:
        x_src = xbc_ref.at[0]
    sh_row = lax.broadcasted_iota(jnp.int32, ((CONV_W - 1) * L, L), 0)
    sh_col = lax.broadcasted_iota(jnp.int32, ((CONV_W - 1) * L, L), 1)
    sh_k = sh_row // L
    shift = jnp.where(sh_col == sh_row - sh_k * L - (CONV_W - 1) + sh_k, 1.0, 0.0).astype(BF16)
    slab = GROUP_CH
    for s in range(CONV_DIM // slab):
        sl = slice(slab * s, slab * (s + 1))
        xb = x_src[:, sl]
        sh = jnp.dot(shift, xb, preferred_element_type=F32)
        acc = cb_ref[:, sl] + cw_ref[3:4, sl] * xb.astype(F32)
        head = jnp.zeros((CONV_PAD, slab), F32)
        for k in range(CONV_W - 1):
            acc = acc + cw_ref[k:k + 1, sl] * sh[k * L:(k + 1) * L]
            head = head + cw_ref[k:k + 1, sl] * xpad_ref[CONV_PAD - 3 + k:2 * CONV_PAD - 3 + k, sl]
        acc = jnp.concatenate([acc[0:CONV_PAD] + head, acc[CONV_PAD:]], axis=0)
        act = _silu(acc).astype(BF16)
        if s < D_INNER // slab:
            xs_ref[:, sl] = act
        else:
            o = slab * s - D_INNER
            bc_ref[:, o:o + slab] = act
    tbase = (lvalid - 3) // 16 * 16
    tail = x_src[tbase:tbase + 16, :].astype(F32)[lvalid - 3 - tbase:lvalid - tbase]
    xpad_ref[CONV_PAD - 3:CONV_PAD, :] = tail

    @pl.when(c == nchunks - 1)
    def _conv_out():
        convt_ref[0] = tail

    if lin < L:
        dtpad_ref[0:lin, :] = dt_ref[0]
        zpad_ref[0:lin, :] = z_ref[0]
        dt_raw = dtpad_ref[...]
        z_src = zpad_ref
    else:
        dt_raw = dt_ref[0]
        z_src = z_ref.at[0]
    row_i = lax.broadcasted_iota(jnp.int32, (L, L), 0)
    col_i = lax.broadcasted_iota(jnp.int32, (L, L), 1)
    causal = col_i <= row_i
    lane_lo = col_i < SSM_HEAD_DIM
    dtv = jax.nn.softplus(dt_raw + dtb_ref[...])
    if lvalid < L:
        dtv = jnp.where(row_i < lvalid, dtv, 0.0)
    a = dtv * (-jnp.exp(alog_ref[...]))
    tri = jnp.where(causal, 1.0, 0.0).astype(BF16)
    acum_col = sum(jnp.dot(tri, part, preferred_element_type=F32) for part in _split3(a))
    acum_row = acum_col.T
    dt_row = dtv.T
    last = jnp.broadcast_to(acum_row[:, L - 1:L], (L, L))
    wd_row = jnp.exp(last - acum_row) * dt_row

    for g in range(SSM_GROUPS):
        bg = bc_ref[:, D_STATE * g:D_STATE * (g + 1)]
        cg = bc_ref[:, GROUP_CH + D_STATE * g:GROUP_CH + D_STATE * (g + 1)]
        cb = lax.dot_general(cg, bg, (((1,), (1,)), ((), ())), preferred_element_type=F32)
        bgt = bg.astype(F32).T
        gsl = slice(GROUP_CH * g, GROUP_CH * (g + 1))
        y_inter = jnp.dot(cg, st_ref[:, gsl].astype(BF16), preferred_element_type=F32)
        outs = []
        for j in range(GROUP_CH // LANES):
            h0 = (GROUP_CH // SSM_HEAD_DIM) * g + 2 * j
            psl = slice(GROUP_CH * g + LANES * j, GROUP_CH * g + LANES * (j + 1))
            acs, lms, wms = [], [], []
            for h in (h0, h0 + 1):
                ac = jnp.broadcast_to(acum_col[:, h:h + 1], (L, L))
                seg = ac - acum_row[h:h + 1, :]
                lm = jnp.where(causal, jnp.exp(seg), 0.0) * cb * dt_row[h:h + 1, :]
                acs.append(ac)
                lms.append(lm.astype(BF16))
                wms.append((bgt * wd_row[h:h + 1, :]).astype(BF16))
            x = xs_ref[:, psl]
            zero = jnp.zeros_like(x)
            xbd = jnp.concatenate([jnp.where(lane_lo, x, zero), jnp.where(lane_lo, zero, x)], axis=0)
            y_intra = jnp.dot(jnp.concatenate(lms, axis=1), xbd, preferred_element_type=F32)
            acp = jnp.where(lane_lo, acs[0], acs[1])
            y = y_intra + y_inter[:, LANES * j:LANES * (j + 1)] * jnp.exp(acp)
            outs.append(y + dsk_ref[:, psl] * x.astype(F32))
            upd = jnp.dot(jnp.concatenate(wms, axis=1), xbd, preferred_element_type=F32)
            st_ref[:, psl] = st_ref[:, psl] * jnp.exp(acp[L - 1:L, :]) + upd
        yg = jnp.concatenate(outs, axis=1)
        zg = z_src[:, gsl].astype(F32)
        gated = yg * _silu(zg)
        ms = jnp.mean(gated * gated, axis=-1, keepdims=True)
        yn = gated * lax.rsqrt(ms + EPS) * ng_ref[:, gsl]
        y_ref[0, :, gsl] = yn[0:lin].astype(y_ref.dtype)

    @pl.when(c == nchunks - 1)
    def _state_out():
        for j in range(n_pairs):
            blk = st_ref[:, LANES * j:LANES * (j + 1)].T
            ssmt_ref[0, 2 * j:2 * j + 2] = blk.reshape(2, SSM_HEAD_DIM, D_STATE)


def _ssd(z, xbc, dt, conv0, ssm0, cw, cb, dtb, alog, dsk, ng, *, lin, lvalid):
    nseq, ltot, _ = z.shape
    nchunks = ltot // lin
    seq_blk = lambda last: pl.BlockSpec((1, lin, last), lambda s, c: (s, c, 0))
    return pl.pallas_call(
        functools.partial(_ssd_kernel, lin=lin, lvalid=lvalid, nchunks=nchunks),
        grid=(nseq, nchunks),
        in_specs=[
            seq_blk(D_INNER), seq_blk(CONV_DIM), seq_blk(LANES),
            pl.BlockSpec((1, CONV_W - 1, CONV_DIM), lambda s, c: (s, 0, 0)),
            pl.BlockSpec((1, SSM_HEADS, SSM_HEAD_DIM, D_STATE), lambda s, c: (s, 0, 0, 0)),
            _const_spec((CONV_W, CONV_DIM)), _const_spec((1, CONV_DIM)),
            _const_spec((1, LANES)), _const_spec((1, LANES)),
            _const_spec((1, D_INNER)), _const_spec((1, D_INNER)),
        ],
        out_specs=[
            seq_blk(D_INNER),
            pl.BlockSpec((1, CONV_W - 1, CONV_DIM), lambda s, c: (s, 0, 0)),
            pl.BlockSpec((1, SSM_HEADS, SSM_HEAD_DIM, D_STATE), lambda s, c: (s, 0, 0, 0)),
        ],
        out_shape=[
            jax.ShapeDtypeStruct((nseq, ltot, D_INNER), BF16),
            jax.ShapeDtypeStruct((nseq, CONV_W - 1, CONV_DIM), F32),
            jax.ShapeDtypeStruct((nseq, SSM_HEADS, SSM_HEAD_DIM, D_STATE), F32),
        ],
        scratch_shapes=[
            pltpu.VMEM((2 * CONV_PAD, CONV_DIM), F32),
            pltpu.VMEM((CHUNK, D_INNER), BF16),
            pltpu.VMEM((CHUNK, 2 * GROUP_CH), BF16),
            pltpu.VMEM((D_STATE, D_INNER), F32),
            pltpu.VMEM((CHUNK, LANES), F32),
            pltpu.VMEM((CHUNK, D_INNER), BF16),
            pltpu.VMEM((CHUNK, CONV_DIM), BF16),
        ],
        compiler_params=_params(2),
        name="ssd",
    )(z, xbc, dt, conv0, ssm0, cw, cb, dtb, alog, dsk, ng)


def _attn_kernel(q_ref, kp_ref, vp_ref, kc_ref, vc_ref, sink_ref, o_ref, *pads, tq, first_has_prev,
                 feature_major):
    i = pl.program_id(1)
    W = WINDOW
    if feature_major:
        qt = q_ref[...]
        kc = kc_ref[0]
        vt = jnp.concatenate([vc_ref[...], vp_ref[...]], axis=1).astype(BF16)
    else:
        qpad_ref, kcpad_ref, vcpad_ref = pads

        @pl.when((pl.program_id(0) == 0) & (i == 0))
        def _zero():
            qpad_ref[...] = jnp.zeros_like(qpad_ref)
            kcpad_ref[...] = jnp.zeros_like(kcpad_ref)
            vcpad_ref[...] = jnp.zeros_like(vcpad_ref)

        qpad_ref[0:tq, :] = q_ref[0].astype(F32)
        kcpad_ref[0:tq, :] = kc_ref[0]
        vcpad_ref[0:tq, :] = vc_ref[0]
        qt = qpad_ref[...].T.astype(BF16)
        kc = kcpad_ref[...]
        vt = jnp.concatenate([vcpad_ref[...].T, vp_ref[0].T], axis=1).astype(BF16)
    qt = qt * (HEAD_DIM ** -0.5)
    kc = kc.astype(BF16)
    kp = kp_ref[0].astype(BF16)
    GW = Q_PER_KV * W
    jj = lax.broadcasted_iota(jnp.int32, (W, GW), 0)
    lane = lax.broadcasted_iota(jnp.int32, (W, GW), 1)
    tt = lane & (W - 1)
    cur = jj <= tt
    if first_has_prev:
        prev = jj > tt
    else:
        prev = jj > tt + jnp.where(i > 0, 0, W)
    lane_row = lax.broadcasted_iota(jnp.int32, (1, GW), 1)
    outs = []
    for g in range(N_KV_HEADS):
        gs = slice(HEAD_DIM * g, HEAD_DIM * (g + 1))
        qg = jnp.concatenate([qt[HEAD_DIM * (Q_PER_KV * g + r):HEAD_DIM * (Q_PER_KV * g + r + 1), :]
                              for r in range(Q_PER_KV)], axis=1)
        s_c = jnp.dot(kc[:, gs], qg, preferred_element_type=F32)
        s_p = jnp.dot(kp[:, gs], qg, preferred_element_type=F32)
        s = jnp.where(cur, s_c, jnp.where(prev, s_p, -jnp.inf))
        sink = jnp.full((1, GW), sink_ref[Q_PER_KV * g], F32)
        for r in range(1, Q_PER_KV):
            sink = jnp.where(lane_row >= r * W, sink_ref[Q_PER_KV * g + r], sink)
        m = jnp.maximum(jnp.max(s, axis=0, keepdims=True), sink)
        p = jnp.exp(s - m)
        denom = jnp.sum(p, axis=0, keepdims=True) + jnp.exp(sink - m)
        zero = jnp.zeros_like(p)
        pcat = jnp.concatenate([jnp.where(cur, p, zero).astype(BF16),
                                jnp.where(cur, zero, p).astype(BF16)], axis=0)
        ot = jnp.dot(vt[gs, :], pcat, preferred_element_type=F32) * (1.0 / denom)
        outs.extend(ot[:, r * W:(r + 1) * W] for r in range(Q_PER_KV))
    o = jnp.concatenate(outs, axis=0).T
    o_ref[0] = o[0:tq].astype(o_ref.dtype)


def _attn_blocks(qt, k, vt, sinks, *, nseq):
    ltot = k.shape[1]
    nblk = ltot // WINDOW
    cur3 = lambda s, i: (s, i, 0)
    prev3 = lambda s, i: (s, jnp.maximum(i - 1, 0), 0)
    cur_t = lambda s, i: (0, s * nblk + i)
    prev_t = lambda s, i: (0, s * nblk + jnp.maximum(i - 1, 0))
    return pl.pallas_call(
        functools.partial(_attn_kernel, tq=WINDOW, first_has_prev=False, feature_major=True),
        grid=(nseq, nblk),
        in_specs=[
            pl.BlockSpec((D_MODEL, WINDOW), cur_t),
            pl.BlockSpec((1, WINDOW, KV_DIM), prev3),
            pl.BlockSpec((KV_DIM, WINDOW), prev_t),
            pl.BlockSpec((1, WINDOW, KV_DIM), cur3),
            pl.BlockSpec((KV_DIM, WINDOW), cur_t),
            pl.BlockSpec(memory_space=pltpu.SMEM),
        ],
        out_specs=pl.BlockSpec((1, WINDOW, D_MODEL), cur3),
        out_shape=jax.ShapeDtypeStruct((nseq, ltot, D_MODEL), BF16),
        compiler_params=_params(2),
        name="attn",
    )(qt, k, vt, k, vt, sinks)


def _attn_step(q, k_prev, v_prev, k_cur, v_cur, sinks):
    nseq, tq, _ = q.shape
    seq3 = lambda s, i: (s, 0, 0)
    return pl.pallas_call(
        functools.partial(_attn_kernel, tq=tq, first_has_prev=True, feature_major=False),
        grid=(nseq, 1),
        in_specs=[
            pl.BlockSpec((1, tq, D_MODEL), seq3),
            pl.BlockSpec((1, WINDOW, KV_DIM), seq3),
            pl.BlockSpec((1, WINDOW, KV_DIM), seq3),
            pl.BlockSpec((1, tq, KV_DIM), seq3),
            pl.BlockSpec((1, tq, KV_DIM), seq3),
            pl.BlockSpec(memory_space=pltpu.SMEM),
        ],
        out_specs=pl.BlockSpec((1, tq, D_MODEL), seq3),
        out_shape=jax.ShapeDtypeStruct((nseq, tq, D_MODEL), BF16),
        scratch_shapes=[pltpu.VMEM((WINDOW, D_MODEL), F32), pltpu.VMEM((WINDOW, KV_DIM), F32),
                        pltpu.VMEM((WINDOW, KV_DIM), F32)],
        compiler_params=_params(2),
        name="attn_step",
    )(q, k_prev, v_prev, k_cur, v_cur, sinks)


def _trunk(x3, conv0, ssm0, k_win, v_win, p, *, lin, lvalid):
    nseq, ltot, _ = x3.shape
    m = nseq * ltot
    x = x3.reshape(m, D_MODEL)
    g = p["norm_gain"]
    x = _ffn(x, g[0, 0], *p["ffn"][0][0])
    z, xbc, dt = _proj(x, g[0, 1], [p["w_z"], p["w_xbc"], p["w_dt"]],
                       [jnp.zeros((D_INNER,), F32), jnp.zeros((CONV_DIM,), F32), jnp.zeros((LANES,), F32)],
                       [BF16, BF16, F32], "in_proj")
    y, conv_t, ssm_t = _ssd(z.reshape(nseq, ltot, D_INNER), xbc.reshape(nseq, ltot, CONV_DIM),
                            dt.reshape(nseq, ltot, LANES), conv0, ssm0, p["conv_w"], p["conv_b"],
                            p["dt_bias"], p["a_log"], p["d_skip"], p["ssm_norm"], lin=lin, lvalid=lvalid)
    x = _matres(y.reshape(m, D_INNER), p["w_out"], jnp.zeros((D_MODEL,), F32), x, "out_proj")
    x = _ffn(x, g[0, 2], *p["ffn"][0][1])
    prompt = k_win is None
    if prompt:
        k, vt = _proj(x, p["kv_norm"], [p["w_k"], p["w_v_t"]], [p["b_k"], p["b_v"]], [F32, F32], "kv_proj",
                      transposed=(False, True))
    else:
        k, v = _proj(x, p["kv_norm"], [p["w_k"], p["w_v"]], [p["b_k"], p["b_v"]], [F32, F32], "kv_proj")
    k3 = k.reshape(nseq, ltot, KV_DIM)
    x = _ffn(x, g[1, 0], *p["ffn"][1][0])
    if prompt:
        (qt,) = _proj(x, g[1, 1], [p["w_q_t"]], [p["b_q"]], [BF16], "q_proj", transposed=(True,))
        o = _attn_blocks(qt, k3, vt, p["sinks"], nseq=nseq)
        v_tail = vt.reshape(KV_DIM, nseq, ltot)[:, :, ltot - WINDOW:].transpose(1, 2, 0)
    else:
        (q,) = _proj(x, g[1, 1], [p["w_q"]], [p["b_q"]], [BF16], "q_proj")
        v_tail = v.reshape(nseq, ltot, KV_DIM)
        o = _attn_step(q.reshape(nseq, ltot, D_MODEL), k_win, v_win, k3, v_tail, p["sinks"])
    x = _matres(o.reshape(m, D_MODEL), p["w_o"], p["b_o"], x, "o_proj")
    y_out = _ffn(x, g[1, 2], *p["ffn"][1][1], final_g=p["final_norm"])
    return y_out.reshape(nseq, ltot, D_MODEL), conv_t, ssm_t, k3, v_tail


def kernel(x_prompt, x_sample, state_conv, state_ssm, cache_k_win, cache_v_win, norm_gain, ffn_w_gate, ffn_w_up, ffn_w_down, ssm_w_in, ssm_conv_w, ssm_conv_b, ssm_dt_bias, ssm_a_log, ssm_d, ssm_norm, ssm_w_out, kv_norm, attn_w_kv, attn_b_kv, attn_w_q, attn_b_q, attn_sinks, attn_w_o, attn_b_o, final_norm):
    depth = ffn_w_gate.shape[0]
    assert depth == 2 and ssm_w_in.shape[0] == 1 and attn_w_q.shape[0] == 1

    def chunked_cols(w):
        return w.astype(BF16).reshape(D_MODEL, N_FF_CHUNKS, FF_CHUNK).transpose(1, 0, 2)

    ffn = [[(chunked_cols(ffn_w_gate[l, i]), chunked_cols(ffn_w_up[l, i]),
             ffn_w_down[l, i].astype(BF16).reshape(N_FF_CHUNKS, FF_CHUNK, D_MODEL))
            for i in range(2)] for l in range(depth)]
    w_in = ssm_w_in[0].astype(BF16)
    pad_heads = lambda v: jnp.pad(v.astype(F32), (0, LANES - SSM_HEADS)).reshape(1, LANES)
    w_kv = attn_w_kv.astype(BF16)
    p = {
        "norm_gain": norm_gain,
        "ffn": ffn,
        "w_z": w_in[:, :D_INNER],
        "w_xbc": w_in[:, D_INNER:D_INNER + CONV_DIM],
        "w_dt": jnp.pad(w_in[:, D_INNER + CONV_DIM:], ((0, 0), (0, LANES - SSM_HEADS))),
        "conv_w": ssm_conv_w[0].astype(F32),
        "conv_b": ssm_conv_b[0].astype(F32).reshape(1, CONV_DIM),
        "dt_bias": pad_heads(ssm_dt_bias[0]),
        "a_log": pad_heads(ssm_a_log[0]),
        "d_skip": jnp.repeat(ssm_d[0].astype(F32), SSM_HEAD_DIM).reshape(1, D_INNER),
        "ssm_norm": ssm_norm[0].astype(F32).reshape(1, D_INNER),
        "w_out": ssm_w_out[0].astype(BF16),
        "kv_norm": kv_norm,
        "w_k": w_kv[:, :KV_DIM], "w_v": w_kv[:, KV_DIM:],
        "b_k": attn_b_kv[:KV_DIM], "b_v": attn_b_kv[KV_DIM:],
        "w_v_t": w_kv[:, KV_DIM:].T,
        "w_q": attn_w_q[0].astype(BF16),
        "w_q_t": attn_w_q[0].astype(BF16).T,
        "b_q": attn_b_q[0],
        "sinks": attn_sinks[0].astype(F32),
        "w_o": attn_w_o[0].astype(BF16),
        "b_o": attn_b_o[0],
        "final_norm": final_norm,
    }

    bsz, seq, _ = x_prompt.shape
    conv0 = jnp.zeros((bsz, CONV_W - 1, CONV_DIM), F32)
    ssm0 = jnp.zeros((bsz, SSM_HEADS, SSM_HEAD_DIM, D_STATE), F32)
    y_p, conv_p, ssm_p, k_p, v_p = _trunk(x_prompt, conv0, ssm0, None, None, p, lin=CHUNK, lvalid=CHUNK)
    k_win_p = k_p[:, seq - WINDOW:].reshape(bsz, WINDOW, N_KV_HEADS, HEAD_DIM)
    v_win_p = v_p.reshape(bsz, WINDOW, N_KV_HEADS, HEAD_DIM)

    dbsz, dseq, _ = x_sample.shape
    assert CONV_W - 1 <= dseq <= 8
    lin_s = 8
    xs = jnp.pad(x_sample, ((0, 0), (0, lin_s - dseq), (0, 0)))
    ck = cache_k_win.reshape(dbsz, WINDOW, KV_DIM)
    cv = cache_v_win.reshape(dbsz, WINDOW, KV_DIM)
    y_s, conv_s, ssm_s, k_s, v_s = _trunk(xs, state_conv[0], state_ssm[0], ck, cv, p, lin=lin_s, lvalid=dseq)
    k_win_s = jnp.concatenate([ck[:, dseq:], k_s[:, :dseq]], axis=1).reshape(dbsz, WINDOW, N_KV_HEADS, HEAD_DIM)
    v_win_s = jnp.concatenate([cv[:, dseq:], v_s[:, :dseq]], axis=1).reshape(dbsz, WINDOW, N_KV_HEADS, HEAD_DIM)
    return (y_p, y_s[:, :dseq], conv_p[None], ssm_p[None], k_win_p, v_win_p,
            conv_s[None], ssm_s[None], k_win_s, v_win_s)
```

```python
import functools

import jax
import jax.numpy as jnp
from jax import lax
from jax.experimental import pallas as pl
from jax.experimental.pallas import tpu as pltpu

F32 = jnp.float32
BF16 = jnp.bfloat16

D_MODEL = 1024
D_INNER = 2048
SSM_HEADS = 32
SSM_HEAD_DIM = 64
SSM_GROUPS = 4
D_STATE = 128
CONV_W = 4
CONV_DIM = D_INNER + 2 * SSM_GROUPS * D_STATE
GROUP_CH = D_INNER // SSM_GROUPS
CHUNK = 128
WINDOW = 128
HEAD_DIM = 64
N_Q_HEADS = 16
N_KV_HEADS = 4
Q_PER_KV = 4
KV_DIM = N_KV_HEADS * HEAD_DIM
D_FF = 2816
EPS = 1e-6

LANES = 128
FF_CHUNK = 256
N_FF_CHUNKS = D_FF // FF_CHUNK
TOKEN_TILE = 512
PROJ_CHUNK = 512
PROJ_T_CHUNK = 256
VMEM_LIMIT = 56 * 1024 * 1024
NT_DIMS = (((1,), (1,)), ((), ()))


def _rms(x, g):
    ms = jnp.mean(x * x, axis=-1, keepdims=True)
    return x * lax.rsqrt(ms + EPS) * g


def _silu(x):
    hx = 0.5 * x
    return hx + hx * jnp.tanh(hx)


def _const_spec(shape):
    nd = len(shape)
    return pl.BlockSpec(shape, lambda *_: (0,) * nd, pipeline_mode=pl.Buffered(1))


def _params(n_axes):
    return pltpu.CompilerParams(dimension_semantics=("arbitrary",) * n_axes,
                                vmem_limit_bytes=VMEM_LIMIT)


def _dot(a, b):
    return jnp.dot(a, b, preferred_element_type=F32)


def _proj_body(hn_ref, w_refs, b_refs, o_refs, transposed):
    tm = hn_ref.shape[0]
    for w_ref, b_ref, o_ref, tr in zip(w_refs, b_refs, o_refs, transposed):
        if tr:
            for c0 in range(0, tm, PROJ_T_CHUNK):
                r = lax.dot_general(w_ref[...], hn_ref[c0:c0 + PROJ_T_CHUNK, :], NT_DIMS,
                                    preferred_element_type=F32)
                o_ref[:, c0:c0 + PROJ_T_CHUNK] = (r + b_ref[...]).astype(o_ref.dtype)
        else:
            n = w_ref.shape[1]
            for c0 in range(0, n, PROJ_CHUNK):
                c1 = min(c0 + PROJ_CHUNK, n)
                r = _dot(hn_ref[...], w_ref[:, c0:c1])
                o_ref[:, c0:c1] = (r + b_ref[:, c0:c1]).astype(o_ref.dtype)


def _proj_operands(ws, bs, dtypes, transposed, m, tm):
    bs = [jnp.broadcast_to(b.astype(F32).reshape(-1, 1), (b.size, PROJ_T_CHUNK)) if tr
          else b.reshape(1, -1).astype(F32) for b, tr in zip(bs, transposed)]
    in_specs = [_const_spec(w.shape) for w in ws] + [_const_spec(b.shape) for b in bs]
    out_specs, out_shape = [], []
    for w, dt, tr in zip(ws, dtypes, transposed):
        if tr:
            out_specs.append(pl.BlockSpec((w.shape[0], tm), lambda i: (0, i)))
            out_shape.append(jax.ShapeDtypeStruct((w.shape[0], m), dt))
        else:
            out_specs.append(pl.BlockSpec((tm, w.shape[1]), lambda i: (i, 0)))
            out_shape.append(jax.ShapeDtypeStruct((m, w.shape[1]), dt))
    return bs, in_specs, out_specs, out_shape


def _proj_kernel(*refs, n_out, transposed):
    x_ref, g_ref = refs[0], refs[1]
    w_refs = refs[2:2 + n_out]
    b_refs = refs[2 + n_out:2 + 2 * n_out]
    o_refs = refs[2 + 2 * n_out:2 + 3 * n_out]
    hn_ref = refs[2 + 3 * n_out]
    hn_ref[...] = _rms(x_ref[...], g_ref[...]).astype(BF16)
    _proj_body(hn_ref, w_refs, b_refs, o_refs, transposed)


def _proj(x, g, ws, bs, dtypes, name):
    m = x.shape[0]
    tm = min(TOKEN_TILE, m)
    transposed = (False,) * len(ws)
    bs, w_specs, out_specs, out_shape = _proj_operands(ws, bs, dtypes, transposed, m, tm)
    return pl.pallas_call(
        functools.partial(_proj_kernel, n_out=len(ws), transposed=transposed),
        grid=(m // tm,),
        in_specs=[pl.BlockSpec((tm, D_MODEL), lambda i: (i, 0)), _const_spec((1, D_MODEL))] + w_specs,
        out_specs=out_specs,
        out_shape=out_shape,
        scratch_shapes=[pltpu.VMEM((tm, D_MODEL), BF16)],
        compiler_params=_params(1),
        name=name,
    )(x, g.reshape(1, D_MODEL), *ws, *bs)


def _ffn_kernel(*refs, pre, final, n_post, post_t):
    it = iter(refs)
    x_ref = next(it)
    if pre:
        a_ref, pw_ref, pb_ref = next(it), next(it), next(it)
    g_ref, wg_ref, wu_ref, wd_ref = next(it), next(it), next(it), next(it)
    fg_ref = next(it) if final else None
    if n_post:
        pg_ref = next(it)
        w_refs = [next(it) for _ in range(n_post)]
        b_refs = [next(it) for _ in range(n_post)]
    o_ref = next(it)
    po_refs = [next(it) for _ in range(n_post)]
    hn_ref, acc_ref = next(it), next(it)
    xin_ref = next(it) if pre else x_ref

    if pre:
        for c0 in range(0, D_MODEL, PROJ_CHUNK):
            cs = slice(c0, c0 + PROJ_CHUNK)
            xin_ref[:, cs] = x_ref[:, cs] + _dot(a_ref[...], pw_ref[:, cs]) + pb_ref[:, cs]
    hn_ref[...] = _rms(xin_ref[...], g_ref[...]).astype(BF16)
    for c in range(N_FF_CHUNKS):
        cs = slice(FF_CHUNK * c, FF_CHUNK * (c + 1))
        hn = hn_ref[...]
        gate = _dot(hn, wg_ref[:, cs])
        up = _dot(hn, wu_ref[:, cs])
        act = (_silu(gate) * up).astype(BF16)
        down = _dot(act, wd_ref[cs, :])
        if c == 0:
            acc_ref[...] = down
        else:
            acc_ref[...] += down
    out = xin_ref[...] + 0.5 * acc_ref[...]
    o_ref[...] = _rms(out, fg_ref[...]) if final else out
    if n_post:
        hn_ref[...] = _rms(out, pg_ref[...]).astype(BF16)
        _proj_body(hn_ref, w_refs, b_refs, po_refs, post_t)


def _ffn(x, g, ffn_w, layer, idx, *, pre=None, final_g=None, post=None, name="ffn"):
    m = x.shape[0]
    tm = min(TOKEN_TILE, m)
    tile = lambda n: pl.BlockSpec((tm, n), lambda i: (i, 0))
    w_spec = lambda r, c: pl.BlockSpec((None, None, r, c), lambda i: (layer, idx, 0, 0),
                                       pipeline_mode=pl.Buffered(1))
    args, in_specs = [x], [tile(D_MODEL)]
    scratch = [pltpu.VMEM((tm, D_MODEL), BF16), pltpu.VMEM((tm, D_MODEL), F32)]
    if pre is not None:
        a, pw, pb = pre
        args += [a, pw, pb.reshape(1, D_MODEL).astype(F32)]
        in_specs += [tile(a.shape[1]), _const_spec(pw.shape), _const_spec((1, D_MODEL))]
        scratch.append(pltpu.VMEM((tm, D_MODEL), F32))
    args += [g.reshape(1, D_MODEL), *ffn_w]
    in_specs += [_const_spec((1, D_MODEL)), w_spec(D_MODEL, D_FF), w_spec(D_MODEL, D_FF), w_spec(D_FF, D_MODEL)]
    if final_g is not None:
        args.append(final_g.reshape(1, D_MODEL))
        in_specs.append(_const_spec((1, D_MODEL)))
    out_specs = [tile(D_MODEL)]
    out_shape = [jax.ShapeDtypeStruct((m, D_MODEL), F32)]
    n_post, post_t = 0, ()
    if post is not None:
        pg, ws, bs, dtypes, post_t = post
        n_post = len(ws)
        bs, w_specs, p_out_specs, p_out_shape = _proj_operands(ws, bs, dtypes, post_t, m, tm)
        args += [pg.reshape(1, D_MODEL), *ws, *bs]
        in_specs += [_const_spec((1, D_MODEL))] + w_specs
        out_specs += p_out_specs
        out_shape += p_out_shape
    return pl.pallas_call(
        functools.partial(_ffn_kernel, pre=pre is not None, final=final_g is not None,
                          n_post=n_post, post_t=tuple(post_t)),
        grid=(m // tm,),
        in_specs=in_specs,
        out_specs=out_specs,
        out_shape=out_shape,
        scratch_shapes=scratch,
        compiler_params=_params(1),
        name=name,
    )(*args)


CONV_PAD = 8


def _inproj_kernel(x_ref, g_ref, wz_ref, wx_ref, wdt_ref, conv0_ref, cw_ref, cb_ref, dtb_ref,
                   zg_ref, xs_ref, bc_ref, dt_ref, convt_ref, hn_ref, stg_ref, carry_ref, *, tiles_per_seq):
    i = pl.program_id(0)
    tm = x_ref.shape[0]
    lo = CONV_PAD - (CONV_W - 1)
    hn_ref[...] = _rms(x_ref[...], g_ref[...]).astype(BF16)

    @pl.when(i % tiles_per_seq == 0)
    def _seq_start():
        carry_ref[lo:CONV_PAD, :] = conv0_ref[0]

    for c0 in range(0, D_INNER, PROJ_CHUNK):
        cs = slice(c0, c0 + PROJ_CHUNK)
        zg_ref[:, cs] = _silu(_dot(hn_ref[...], wz_ref[:, cs])).astype(BF16)
    dt_ref[...] = jax.nn.softplus(_dot(hn_ref[...], wdt_ref[...]) + dtb_ref[...])
    for c0 in range(0, CONV_DIM, PROJ_CHUNK):
        cs = slice(c0, c0 + PROJ_CHUNK)
        r = _dot(hn_ref[...], wx_ref[:, cs])
        stg_ref[lo:CONV_PAD, :] = carry_ref[lo:CONV_PAD, cs]
        stg_ref[CONV_PAD:CONV_PAD + tm, :] = r
        acc = cb_ref[:, cs] + cw_ref[CONV_W - 1:CONV_W, cs] * r
        for k in range(CONV_W - 1):
            acc = acc + cw_ref[k:k + 1, cs] * stg_ref[lo + k:lo + k + tm, :]
        carry_ref[lo:CONV_PAD, cs] = stg_ref[lo + tm:CONV_PAD + tm, :]
        act = _silu(acc).astype(BF16)
        if c0 < D_INNER:
            xs_ref[:, cs] = act
        else:
            bc_ref[:, c0 - D_INNER:c0 - D_INNER + PROJ_CHUNK] = act

    @pl.when(i % tiles_per_seq == tiles_per_seq - 1)
    def _seq_end():
        convt_ref[0] = carry_ref[lo:CONV_PAD, :]


def _inproj_conv(x, g, wz, wx, wdt, conv0, cw, cb, dtb, *, nseq):
    m = x.shape[0]
    tm = TOKEN_TILE
    tiles_per_seq = m // nseq // tm
    tile = lambda n: pl.BlockSpec((tm, n), lambda i: (i, 0))
    seq_conv = pl.BlockSpec((1, CONV_W - 1, CONV_DIM), lambda i: (i // tiles_per_seq, 0, 0))
    return pl.pallas_call(
        functools.partial(_inproj_kernel, tiles_per_seq=tiles_per_seq),
        grid=(m // tm,),
        in_specs=[tile(D_MODEL), _const_spec((1, D_MODEL)), _const_spec(wz.shape), _const_spec(wx.shape),
                  _const_spec(wdt.shape), seq_conv, _const_spec(cw.shape), _const_spec(cb.shape),
                  _const_spec(dtb.shape)],
        out_specs=[tile(D_INNER), tile(D_INNER), tile(2 * GROUP_CH), tile(LANES), seq_conv],
        out_shape=[jax.ShapeDtypeStruct((m, D_INNER), BF16), jax.ShapeDtypeStruct((m, D_INNER), BF16),
                   jax.ShapeDtypeStruct((m, 2 * GROUP_CH), BF16), jax.ShapeDtypeStruct((m, LANES), F32),
                   jax.ShapeDtypeStruct((nseq, CONV_W - 1, CONV_DIM), F32)],
        scratch_shapes=[pltpu.VMEM((tm, D_MODEL), BF16), pltpu.VMEM((CONV_PAD + tm, PROJ_CHUNK), F32),
                        pltpu.VMEM((CONV_PAD, CONV_DIM), F32)],
        compiler_params=_params(1),
        name="in_proj_conv",
    )(x, g.reshape(1, D_MODEL), wz, wx, wdt, conv0, cw, cb, dtb)


def _split3(v):
    hi = v.astype(BF16)
    r1 = v - hi.astype(F32)
    mid = r1.astype(BF16)
    lo = (r1 - mid.astype(F32)).astype(BF16)
    return hi, mid, lo


def _state_in(ssm0_ref, st_ref):
    for j in range(SSM_HEADS // 2):
        blk = ssm0_ref[0, 2 * j:2 * j + 2].reshape(LANES, D_STATE)
        st_ref[:, LANES * j:LANES * (j + 1)] = blk.T


def _state_out(st_ref, ssmt_ref):
    for j in range(SSM_HEADS // 2):
        blk = st_ref[:, LANES * j:LANES * (j + 1)].T
        ssmt_ref[0, 2 * j:2 * j + 2] = blk.reshape(2, SSM_HEAD_DIM, D_STATE)


def _ssd_chunk(xs_src, bc_src, dtv, gate_fn, alog_ref, dsk_ref, ng_ref, st_ref, y_ref, rows):
    L = CHUNK
    row_i = lax.broadcasted_iota(jnp.int32, (L, L), 0)
    col_i = lax.broadcasted_iota(jnp.int32, (L, L), 1)
    causal = col_i <= row_i
    lane_lo = col_i < SSM_HEAD_DIM
    a = dtv * (-jnp.exp(alog_ref[...]))
    tri = jnp.where(causal, 1.0, 0.0).astype(BF16)
    acum_col = sum(_dot(tri, part) for part in _split3(a))
    acum_row = acum_col.T
    dt_row = dtv.T
    last = jnp.broadcast_to(acum_row[:, L - 1:L], (L, L))
    wd_row = jnp.exp(last - acum_row) * dt_row

    for g in range(SSM_GROUPS):
        bg = bc_src[:, D_STATE * g:D_STATE * (g + 1)]
        cg = bc_src[:, GROUP_CH + D_STATE * g:GROUP_CH + D_STATE * (g + 1)]
        cb = lax.dot_general(cg, bg, NT_DIMS, preferred_element_type=F32)
        bgt = bg.astype(F32).T
        gsl = slice(GROUP_CH * g, GROUP_CH * (g + 1))
        y_inter = _dot(cg, st_ref[:, gsl].astype(BF16))
        outs = []
        for j in range(GROUP_CH // LANES):
            h0 = (GROUP_CH // SSM_HEAD_DIM) * g + 2 * j
            psl = slice(GROUP_CH * g + LANES * j, GROUP_CH * g + LANES * (j + 1))
            acs, lms, wms = [], [], []
            for h in (h0, h0 + 1):
                ac = jnp.broadcast_to(acum_col[:, h:h + 1], (L, L))
                seg = ac - acum_row[h:h + 1, :]
                lm = jnp.where(causal, jnp.exp(seg), 0.0) * cb * dt_row[h:h + 1, :]
                acs.append(ac)
                lms.append(lm.astype(BF16))
                wms.append((bgt * wd_row[h:h + 1, :]).astype(BF16))
            x = xs_src[:, psl]
            zero = jnp.zeros_like(x)
            xbd = jnp.concatenate([jnp.where(lane_lo, x, zero), jnp.where(lane_lo, zero, x)], axis=0)
            y_intra = _dot(jnp.concatenate(lms, axis=1), xbd)
            acp = jnp.where(lane_lo, acs[0], acs[1])
            y = y_intra + y_inter[:, LANES * j:LANES * (j + 1)] * jnp.exp(acp)
            outs.append(y + dsk_ref[:, psl] * x.astype(F32))
            upd = _dot(jnp.concatenate(wms, axis=1), xbd)
            st_ref[:, psl] = st_ref[:, psl] * jnp.exp(acp[L - 1:L, :]) + upd
        gated = jnp.concatenate(outs, axis=1) * gate_fn(gsl)
        ms = jnp.mean(gated * gated, axis=-1, keepdims=True)
        yn = gated * lax.rsqrt(ms + EPS) * ng_ref[:, gsl]
        y_ref[0, :, gsl] = yn[0:rows].astype(y_ref.dtype)


def _ssd_fused_kernel(zg_ref, xs_ref, bc_ref, dt_ref, ssm0_ref, alog_ref, dsk_ref, ng_ref,
                      y_ref, ssmt_ref, st_ref, *, nchunks):
    c = pl.program_id(1)

    @pl.when(c == 0)
    def _init():
        _state_in(ssm0_ref, st_ref)

    _ssd_chunk(xs_ref.at[0], bc_ref.at[0], dt_ref[0], lambda sl: zg_ref[0, :, sl].astype(F32),
               alog_ref, dsk_ref, ng_ref, st_ref, y_ref, CHUNK)

    @pl.when(c == nchunks - 1)
    def _fin():
        _state_out(st_ref, ssmt_ref)


def _ssd_fused(zg, xs, bc, dt, ssm0, alog, dsk, ng):
    nseq, ltot, _ = zg.shape
    nchunks = ltot // CHUNK
    seq_blk = lambda n: pl.BlockSpec((1, CHUNK, n), lambda s, c: (s, c, 0))
    state_blk = pl.BlockSpec((1, SSM_HEADS, SSM_HEAD_DIM, D_STATE), lambda s, c: (s, 0, 0, 0))
    return pl.pallas_call(
        functools.partial(_ssd_fused_kernel, nchunks=nchunks),
        grid=(nseq, nchunks),
        in_specs=[seq_blk(D_INNER), seq_blk(D_INNER), seq_blk(2 * GROUP_CH), seq_blk(LANES), state_blk,
                  _const_spec((1, LANES)), _const_spec((1, D_INNER)), _const_spec((1, D_INNER))],
        out_specs=[seq_blk(D_INNER), state_blk],
        out_shape=[jax.ShapeDtypeStruct((nseq, ltot, D_INNER), BF16),
                   jax.ShapeDtypeStruct((nseq, SSM_HEADS, SSM_HEAD_DIM, D_STATE), F32)],
        scratch_shapes=[pltpu.VMEM((D_STATE, D_INNER), F32)],
        compiler_params=_params(2),
        name="ssd",
    )(zg, xs, bc, dt, ssm0, alog, dsk, ng)


def _ssd_step_kernel(z_ref, xbc_ref, dt_ref, conv0_ref, ssm0_ref, cw_ref, cb_ref, dtb_ref, alog_ref,
                     dsk_ref, ng_ref, y_ref, convt_ref, ssmt_ref,
                     xpad_ref, xs_ref, bc_ref, st_ref, dtpad_ref, zpad_ref, xcur_ref, *, lin, lvalid):
    L = CHUNK
    assert CONV_W - 1 <= lvalid <= lin < L and (lvalid - 1) // 16 == (lvalid - 3) // 16

    @pl.when(pl.program_id(0) == 0)
    def _zero():
        xpad_ref[...] = jnp.zeros_like(xpad_ref)
        xcur_ref[...] = jnp.zeros_like(xcur_ref)
        dtpad_ref[...] = jnp.zeros_like(dtpad_ref)
        zpad_ref[...] = jnp.zeros_like(zpad_ref)

    xpad_ref[CONV_PAD - 3:CONV_PAD, :] = conv0_ref[0]
    _state_in(ssm0_ref, st_ref)

    xcur_ref[0:lin, :] = xbc_ref[0]
    sh_row = lax.broadcasted_iota(jnp.int32, ((CONV_W - 1) * L, L), 0)
    sh_col = lax.broadcasted_iota(jnp.int32, ((CONV_W - 1) * L, L), 1)
    sh_k = sh_row // L
    shift = jnp.where(sh_col == sh_row - sh_k * L - (CONV_W - 1) + sh_k, 1.0, 0.0).astype(BF16)
    slab = GROUP_CH
    for s in range(CONV_DIM // slab):
        sl = slice(slab * s, slab * (s + 1))
        xb = xcur_ref[:, sl]
        sh = _dot(shift, xb)
        acc = cb_ref[:, sl] + cw_ref[3:4, sl] * xb.astype(F32)
        head = jnp.zeros((CONV_PAD, slab), F32)
        for k in range(CONV_W - 1):
            acc = acc + cw_ref[k:k + 1, sl] * sh[k * L:(k + 1) * L]
            head = head + cw_ref[k:k + 1, sl] * xpad_ref[CONV_PAD - 3 + k:2 * CONV_PAD - 3 + k, sl]
        acc = jnp.concatenate([acc[0:CONV_PAD] + head, acc[CONV_PAD:]], axis=0)
        act = _silu(acc).astype(BF16)
        if s < D_INNER // slab:
            xs_ref[:, sl] = act
        else:
            o = slab * s - D_INNER
            bc_ref[:, o:o + slab] = act
    tbase = (lvalid - 3) // 16 * 16
    convt_ref[0] = xcur_ref[tbase:tbase + 16, :].astype(F32)[lvalid - 3 - tbase:lvalid - tbase]

    dtpad_ref[0:lin, :] = dt_ref[0]
    zpad_ref[0:lin, :] = z_ref[0]
    row_i = lax.broadcasted_iota(jnp.int32, (L, LANES), 0)
    dtv = jnp.where(row_i < lvalid, jax.nn.softplus(dtpad_ref[...] + dtb_ref[...]), 0.0)
    _ssd_chunk(xs_ref, bc_ref, dtv, lambda sl: _silu(zpad_ref[:, sl].astype(F32)),
               alog_ref, dsk_ref, ng_ref, st_ref, y_ref, lin)
    _state_out(st_ref, ssmt_ref)


def _ssd_step(z, xbc, dt, conv0, ssm0, cw, cb, dtb, alog, dsk, ng, *, lvalid):
    nseq, lin, _ = z.shape
    seq_blk = lambda r, n: pl.BlockSpec((1, r, n), lambda s: (s, 0, 0))
    state_blk = pl.BlockSpec((1, SSM_HEADS, SSM_HEAD_DIM, D_STATE), lambda s: (s, 0, 0, 0))
    return pl.pallas_call(
        functools.partial(_ssd_step_kernel, lin=lin, lvalid=lvalid),
        grid=(nseq,),
        in_specs=[
            seq_blk(lin, D_INNER), seq_blk(lin, CONV_DIM), seq_blk(lin, LANES),
            seq_blk(CONV_W - 1, CONV_DIM), state_blk,
            _const_spec((CONV_W, CONV_DIM)), _const_spec((1, CONV_DIM)),
            _const_spec((1, LANES)), _const_spec((1, LANES)),
            _const_spec((1, D_INNER)), _const_spec((1, D_INNER)),
        ],
        out_specs=[seq_blk(lin, D_INNER), seq_blk(CONV_W - 1, CONV_DIM), state_blk],
        out_shape=[
            jax.ShapeDtypeStruct((nseq, lin, D_INNER), BF16),
            jax.ShapeDtypeStruct((nseq, CONV_W - 1, CONV_DIM), F32),
            jax.ShapeDtypeStruct((nseq, SSM_HEADS, SSM_HEAD_DIM, D_STATE), F32),
        ],
        scratch_shapes=[
            pltpu.VMEM((2 * CONV_PAD, CONV_DIM), F32),
            pltpu.VMEM((CHUNK, D_INNER), BF16),
            pltpu.VMEM((CHUNK, 2 * GROUP_CH), BF16),
            pltpu.VMEM((D_STATE, D_INNER), F32),
            pltpu.VMEM((CHUNK, LANES), F32),
            pltpu.VMEM((CHUNK, D_INNER), BF16),
            pltpu.VMEM((CHUNK, CONV_DIM), BF16),
        ],
        compiler_params=_params(1),
        name="ssd_step",
    )(z, xbc, dt, conv0, ssm0, cw, cb, dtb, alog, dsk, ng)


def _attn_core(qt, kc, kp, vt, sink_ref, prev_offset):
    W = WINDOW
    GW = Q_PER_KV * W
    qt = qt * (HEAD_DIM ** -0.5)
    jj = lax.broadcasted_iota(jnp.int32, (W, GW), 0)
    tt = lax.broadcasted_iota(jnp.int32, (W, GW), 1) & (W - 1)
    cur = jj <= tt
    prev = jj > tt + prev_offset
    lane_row = lax.broadcasted_iota(jnp.int32, (1, GW), 1)
    outs = []
    for g in range(N_KV_HEADS):
        gs = slice(HEAD_DIM * g, HEAD_DIM * (g + 1))
        qg = jnp.concatenate([qt[HEAD_DIM * (Q_PER_KV * g + r):HEAD_DIM * (Q_PER_KV * g + r + 1), :]
                              for r in range(Q_PER_KV)], axis=1)
        s = jnp.where(cur, _dot(kc[:, gs], qg), jnp.where(prev, _dot(kp[:, gs], qg), -jnp.inf))
        sink = jnp.full((1, GW), sink_ref[Q_PER_KV * g], F32)
        for r in range(1, Q_PER_KV):
            sink = jnp.where(lane_row >= r * W, sink_ref[Q_PER_KV * g + r], sink)
        m = jnp.maximum(jnp.max(s, axis=0, keepdims=True), sink)
        p = jnp.exp(s - m)
        denom = jnp.sum(p, axis=0, keepdims=True) + jnp.exp(sink - m)
        zero = jnp.zeros_like(p)
        pcat = jnp.concatenate([jnp.where(cur, p, zero).astype(BF16),
                                jnp.where(cur, zero, p).astype(BF16)], axis=0)
        ot = _dot(vt[gs, :], pcat) * (1.0 / denom)
        outs.extend(ot[:, r * W:(r + 1) * W] for r in range(Q_PER_KV))
    return jnp.concatenate(outs, axis=0).T


def _attn_blocks_kernel(qt_ref, kp_ref, vtp_ref, kc_ref, vtc_ref, sink_ref, o_ref):
    i = pl.program_id(1)
    vt = jnp.concatenate([vtc_ref[...], vtp_ref[...]], axis=1).astype(BF16)
    o = _attn_core(qt_ref[...], kc_ref[0].astype(BF16), kp_ref[0].astype(BF16), vt, sink_ref,
                   jnp.where(i > 0, 0, WINDOW))
    o_ref[0] = o.astype(o_ref.dtype)


def _attn_blocks(qt, k, vt, sinks, *, nseq):
    ltot = k.shape[1]
    nblk = ltot // WINDOW
    cur3 = lambda s, i: (s, i, 0)
    prev3 = lambda s, i: (s, jnp.maximum(i - 1, 0), 0)
    cur_t = lambda s, i: (0, s * nblk + i)
    prev_t = lambda s, i: (0, s * nblk + jnp.maximum(i - 1, 0))
    return pl.pallas_call(
        _attn_blocks_kernel,
        grid=(nseq, nblk),
        in_specs=[
            pl.BlockSpec((D_MODEL, WINDOW), cur_t),
            pl.BlockSpec((1, WINDOW, KV_DIM), prev3),
            pl.BlockSpec((KV_DIM, WINDOW), prev_t),
            pl.BlockSpec((1, WINDOW, KV_DIM), cur3),
            pl.BlockSpec((KV_DIM, WINDOW), cur_t),
            pl.BlockSpec(memory_space=pltpu.SMEM),
        ],
        out_specs=pl.BlockSpec((1, WINDOW, D_MODEL), cur3),
        out_shape=jax.ShapeDtypeStruct((nseq, ltot, D_MODEL), BF16),
        compiler_params=_params(2),
        name="attn",
    )(qt, k, vt, k, vt, sinks)


def _attn_step_kernel(q_ref, ck_ref, cv_ref, kn_ref, vn_ref, sink_ref, o_ref, kw_ref, vw_ref,
                      qpad_ref, kcpad_ref, vcpad_ref, *, tq, valid):
    W = WINDOW

    @pl.when(pl.program_id(0) == 0)
    def _zero():
        qpad_ref[...] = jnp.zeros_like(qpad_ref)
        kcpad_ref[...] = jnp.zeros_like(kcpad_ref)
        vcpad_ref[...] = jnp.zeros_like(vcpad_ref)

    qpad_ref[0:tq, :] = q_ref[0].astype(F32)
    kcpad_ref[0:tq, :] = kn_ref[0]
    vcpad_ref[0:tq, :] = vn_ref[0]
    kp = jnp.concatenate([ck_ref[0, :, g, :] for g in range(N_KV_HEADS)], axis=1)
    vp = jnp.concatenate([cv_ref[0, :, g, :] for g in range(N_KV_HEADS)], axis=1)
    vt = jnp.concatenate([vcpad_ref[...].T, vp.T], axis=1).astype(BF16)
    o = _attn_core(qpad_ref[...].T.astype(BF16), kcpad_ref[...].astype(BF16), kp.astype(BF16), vt,
                   sink_ref, 0)
    o_ref[0] = o[0:tq].astype(o_ref.dtype)

    kw_ref[0, 0:W - valid] = ck_ref[0, valid:W]
    vw_ref[0, 0:W - valid] = cv_ref[0, valid:W]
    for g in range(N_KV_HEADS):
        gs = slice(HEAD_DIM * g, HEAD_DIM * (g + 1))
        kw_ref[0, W - valid:W, g, :] = kn_ref[0, 0:valid, gs]
        vw_ref[0, W - valid:W, g, :] = vn_ref[0, 0:valid, gs]


def _attn_step(q, cache_k, cache_v, k_new, v_new, sinks, *, valid):
    nseq, tq, _ = q.shape
    seq3 = lambda r, n: pl.BlockSpec((1, r, n), lambda s: (s, 0, 0))
    win = pl.BlockSpec((1, WINDOW, N_KV_HEADS, HEAD_DIM), lambda s: (s, 0, 0, 0))
    win_shape = jax.ShapeDtypeStruct((nseq, WINDOW, N_KV_HEADS, HEAD_DIM), F32)
    return pl.pallas_call(
        functools.partial(_attn_step_kernel, tq=tq, valid=valid),
        grid=(nseq,),
        in_specs=[seq3(tq, D_MODEL), win, win, seq3(tq, KV_DIM), seq3(tq, KV_DIM),
                  pl.BlockSpec(memory_space=pltpu.SMEM)],
        out_specs=[seq3(tq, D_MODEL), win, win],
        out_shape=[jax.ShapeDtypeStruct((nseq, tq, D_MODEL), BF16), win_shape, win_shape],
        scratch_shapes=[pltpu.VMEM((WINDOW, D_MODEL), F32), pltpu.VMEM((WINDOW, KV_DIM), F32),
                        pltpu.VMEM((WINDOW, KV_DIM), F32)],
        compiler_params=_params(1),
        name="attn_step",
    )(q, cache_k, cache_v, k_new, v_new, sinks)


def _trunk_prompt(x3, p):
    nseq, ltot, _ = x3.shape
    m = nseq * ltot
    g, ffn_w = p["norm_gain"], p["ffn"]
    x = x3.reshape(m, D_MODEL)
    (x,) = _ffn(x, g[0, 0], ffn_w, 0, 0)
    conv0 = jnp.zeros((nseq, CONV_W - 1, CONV_DIM), F32)
    ssm0 = jnp.zeros((nseq, SSM_HEADS, SSM_HEAD_DIM, D_STATE), F32)
    zg, xs, bc, dt, conv_t = _inproj_conv(x, g[0, 1], p["w_z"], p["w_xbc"], p["w_dt"], conv0,
                                          p["conv_w"], p["conv_b"], p["dt_bias"], nseq=nseq)
    seq = lambda t: t.reshape(nseq, ltot, t.shape[-1])
    y, ssm_t = _ssd_fused(seq(zg), seq(xs), seq(bc), seq(dt), ssm0, p["a_log"], p["d_skip"], p["ssm_norm"])
    x, k, vt = _ffn(x, g[0, 2], ffn_w, 0, 1, pre=(y.reshape(m, D_INNER), p["w_out"], jnp.zeros((D_MODEL,), F32)),
                    post=(p["kv_norm"], [p["w_k"], p["w_v_t"]], [p["b_k"], p["b_v"]], [F32, F32], (False, True)),
                    name="ffn_outproj_kv")
    x, qt = _ffn(x, g[1, 0], ffn_w, 1, 0, post=(g[1, 1], [p["w_q_t"]], [p["b_q"]], [BF16], (True,)),
                 name="ffn_q")
    k3 = k.reshape(nseq, ltot, KV_DIM)
    o = _attn_blocks(qt, k3, vt, p["sinks"], nseq=nseq)
    (y_out,) = _ffn(x, g[1, 2], ffn_w, 1, 1, pre=(o.reshape(m, D_MODEL), p["w_o"], p["b_o"]),
                    final_g=p["final_norm"], name="ffn_oproj_final")
    win = lambda t: t.reshape(nseq, WINDOW, N_KV_HEADS, HEAD_DIM)
    v_tail = jnp.stack([vt[:, (s + 1) * ltot - WINDOW:(s + 1) * ltot].T for s in range(nseq)])
    return y_out.reshape(nseq, ltot, D_MODEL), conv_t, ssm_t, win(k3[:, ltot - WINDOW:]), win(v_tail)


def _trunk_step(x3, conv0, ssm0, cache_k, cache_v, p, *, valid):
    nseq, lin, _ = x3.shape
    m = nseq * lin
    g, ffn_w = p["norm_gain"], p["ffn"]
    x = x3.reshape(m, D_MODEL)
    (x,) = _ffn(x, g[0, 0], ffn_w, 0, 0)
    z, xbc, dt = _proj(x, g[0, 1], [p["w_z"], p["w_xbc"], p["w_dt"]],
                       [jnp.zeros((D_INNER,), F32), jnp.zeros((CONV_DIM,), F32), jnp.zeros((LANES,), F32)],
                       [BF16, BF16, F32], "in_proj")
    seq = lambda t: t.reshape(nseq, lin, t.shape[-1])
    y, conv_t, ssm_t = _ssd_step(seq(z), seq(xbc), seq(dt), conv0, ssm0, p["conv_w"], p["conv_b"],
                                 p["dt_bias"], p["a_log"], p["d_skip"], p["ssm_norm"], lvalid=valid)
    x, k, v = _ffn(x, g[0, 2], ffn_w, 0, 1, pre=(y.reshape(m, D_INNER), p["w_out"], jnp.zeros((D_MODEL,), F32)),
                   post=(p["kv_norm"], [p["w_k"], p["w_v"]], [p["b_k"], p["b_v"]], [F32, F32], (False, False)),
                   name="ffn_outproj_kv")
    x, q = _ffn(x, g[1, 0], ffn_w, 1, 0, post=(g[1, 1], [p["w_q"]], [p["b_q"]], [BF16], (False,)),
                name="ffn_q")
    o, k_win, v_win = _attn_step(seq(q), cache_k, cache_v, seq(k), seq(v), p["sinks"], valid=valid)
    (y_out,) = _ffn(x, g[1, 2], ffn_w, 1, 1, pre=(o.reshape(m, D_MODEL), p["w_o"], p["b_o"]),
                    final_g=p["final_norm"], name="ffn_oproj_final")
    return y_out.reshape(nseq, lin, D_MODEL), conv_t, ssm_t, k_win, v_win


def kernel(x_prompt, x_sample, state_conv, state_ssm, cache_k_win, cache_v_win, norm_gain, ffn_w_gate, ffn_w_up, ffn_w_down, ssm_w_in, ssm_conv_w, ssm_conv_b, ssm_dt_bias, ssm_a_log, ssm_d, ssm_norm, ssm_w_out, kv_norm, attn_w_kv, attn_b_kv, attn_w_q, attn_b_q, attn_sinks, attn_w_o, attn_b_o, final_norm):
    assert ffn_w_gate.shape[0] == 2 and ssm_w_in.shape[0] == 1 and attn_w_q.shape[0] == 1
    w_in = ssm_w_in[0].astype(BF16)
    pad_heads = lambda v: jnp.pad(v.astype(F32), (0, LANES - SSM_HEADS)).reshape(1, LANES)
    w_kv = attn_w_kv.astype(BF16)
    w_q = attn_w_q[0].astype(BF16)
    p = {
        "norm_gain": norm_gain,
        "ffn": (ffn_w_gate.astype(BF16), ffn_w_up.astype(BF16), ffn_w_down.astype(BF16)),
        "w_z": w_in[:, :D_INNER],
        "w_xbc": w_in[:, D_INNER:D_INNER + CONV_DIM],
        "w_dt": jnp.pad(w_in[:, D_INNER + CONV_DIM:], ((0, 0), (0, LANES - SSM_HEADS))),
        "conv_w": ssm_conv_w[0].astype(F32),
        "conv_b": ssm_conv_b[0].astype(F32).reshape(1, CONV_DIM),
        "dt_bias": pad_heads(ssm_dt_bias[0]),
        "a_log": pad_heads(ssm_a_log[0]),
        "d_skip": jnp.repeat(ssm_d[0].astype(F32), SSM_HEAD_DIM).reshape(1, D_INNER),
        "ssm_norm": ssm_norm[0].astype(F32).reshape(1, D_INNER),
        "w_out": ssm_w_out[0].astype(BF16),
        "kv_norm": kv_norm,
        "w_k": w_kv[:, :KV_DIM], "w_v": w_kv[:, KV_DIM:], "w_v_t": w_kv[:, KV_DIM:].T,
        "b_k": attn_b_kv[:KV_DIM], "b_v": attn_b_kv[KV_DIM:],
        "w_q": w_q, "w_q_t": w_q.T,
        "b_q": attn_b_q[0],
        "sinks": attn_sinks[0].astype(F32),
        "w_o": attn_w_o[0].astype(BF16),
        "b_o": attn_b_o[0],
        "final_norm": final_norm,
    }

    y_p, conv_p, ssm_p, k_win_p, v_win_p = _trunk_prompt(x_prompt, p)

    dbsz, dseq, _ = x_sample.shape
    lin_s = 8
    assert CONV_W - 1 <= dseq <= lin_s
    xs = jnp.pad(x_sample, ((0, 0), (0, lin_s - dseq), (0, 0)))
    y_s, conv_s, ssm_s, k_win_s, v_win_s = _trunk_step(xs, state_conv[0], state_ssm[0], cache_k_win,
                                                       cache_v_win, p, valid=dseq)
    return (y_p, y_s[:, :dseq], conv_p[None], ssm_p[None], k_win_p, v_win_p,
            conv_s[None], ssm_s[None], k_win_s, v_win_s)
```

```python
import functools

import jax
import jax.numpy as jnp
from jax import lax
from jax.experimental import pallas as pl
from jax.experimental.pallas import tpu as pltpu

F32 = jnp.float32
BF16 = jnp.bfloat16

D_MODEL = 1024
D_INNER = 2048
SSM_HEADS = 32
SSM_HEAD_DIM = 64
SSM_GROUPS = 4
D_STATE = 128
CONV_W = 4
CONV_DIM = D_INNER + 2 * SSM_GROUPS * D_STATE
GROUP_CH = D_INNER // SSM_GROUPS
CHUNK = 128
WINDOW = 128
HEAD_DIM = 64
N_Q_HEADS = 16
N_KV_HEADS = 4
Q_PER_KV = 4
KV_DIM = N_KV_HEADS * HEAD_DIM
D_FF = 2816
EPS = 1e-6

LANES = 128
FF_CHUNK = 256
N_FF_CHUNKS = D_FF // FF_CHUNK
TOKEN_TILE = 512
PROJ_CHUNK = 512
PROJ_T_CHUNK = 256
VMEM_LIMIT = 56 * 1024 * 1024
NT_DIMS = (((1,), (1,)), ((), ()))


def _rms(x, g):
    ms = jnp.mean(x * x, axis=-1, keepdims=True)
    return x * lax.rsqrt(ms + EPS) * g


def _silu(x):
    hx = 0.5 * x
    return hx + hx * jnp.tanh(hx)


def _const_spec(shape):
    nd = len(shape)
    return pl.BlockSpec(shape, lambda *_: (0,) * nd, pipeline_mode=pl.Buffered(1))


def _params(n_axes):
    return pltpu.CompilerParams(dimension_semantics=("arbitrary",) * n_axes,
                                vmem_limit_bytes=VMEM_LIMIT)


def _dot(a, b):
    return jnp.dot(a, b, preferred_element_type=F32)


def _proj_body(hn_ref, w_refs, b_refs, o_refs, transposed):
    tm = hn_ref.shape[0]
    for w_ref, b_ref, o_ref, tr in zip(w_refs, b_refs, o_refs, transposed):
        if tr:
            for c0 in range(0, tm, PROJ_T_CHUNK):
                r = lax.dot_general(w_ref[...], hn_ref[c0:c0 + PROJ_T_CHUNK, :], NT_DIMS,
                                    preferred_element_type=F32)
                o_ref[:, c0:c0 + PROJ_T_CHUNK] = (r + b_ref[...]).astype(o_ref.dtype)
        else:
            n = w_ref.shape[1]
            for c0 in range(0, n, PROJ_CHUNK):
                c1 = min(c0 + PROJ_CHUNK, n)
                r = _dot(hn_ref[...], w_ref[:, c0:c1])
                o_ref[:, c0:c1] = (r + b_ref[:, c0:c1]).astype(o_ref.dtype)


def _proj_operands(ws, bs, dtypes, transposed, m, tm):
    bs = [jnp.broadcast_to(b.astype(F32).reshape(-1, 1), (b.size, PROJ_T_CHUNK)) if tr
          else b.reshape(1, -1).astype(F32) for b, tr in zip(bs, transposed)]
    in_specs = [_const_spec(w.shape) for w in ws] + [_const_spec(b.shape) for b in bs]
    out_specs, out_shape = [], []
    for w, dt, tr in zip(ws, dtypes, transposed):
        if tr:
            out_specs.append(pl.BlockSpec((w.shape[0], tm), lambda i: (0, i)))
            out_shape.append(jax.ShapeDtypeStruct((w.shape[0], m), dt))
        else:
            out_specs.append(pl.BlockSpec((tm, w.shape[1]), lambda i: (i, 0)))
            out_shape.append(jax.ShapeDtypeStruct((m, w.shape[1]), dt))
    return bs, in_specs, out_specs, out_shape


def _proj_kernel(*refs, n_out, transposed):
    x_ref, g_ref = refs[0], refs[1]
    w_refs = refs[2:2 + n_out]
    b_refs = refs[2 + n_out:2 + 2 * n_out]
    o_refs = refs[2 + 2 * n_out:2 + 3 * n_out]
    hn_ref = refs[2 + 3 * n_out]
    hn_ref[...] = _rms(x_ref[...], g_ref[...]).astype(BF16)
    _proj_body(hn_ref, w_refs, b_refs, o_refs, transposed)


def _proj(x, g, ws, bs, dtypes, name):
    m = x.shape[0]
    tm = min(TOKEN_TILE, m)
    transposed = (False,) * len(ws)
    bs, w_specs, out_specs, out_shape = _proj_operands(ws, bs, dtypes, transposed, m, tm)
    return pl.pallas_call(
        functools.partial(_proj_kernel, n_out=len(ws), transposed=transposed),
        grid=(m // tm,),
        in_specs=[pl.BlockSpec((tm, D_MODEL), lambda i: (i, 0)), _const_spec((1, D_MODEL))] + w_specs,
        out_specs=out_specs,
        out_shape=out_shape,
        scratch_shapes=[pltpu.VMEM((tm, D_MODEL), BF16)],
        compiler_params=_params(1),
        name=name,
    )(x, g.reshape(1, D_MODEL), *ws, *bs)


def _ffn_kernel(*refs, pre, final, n_post, post_t):
    it = iter(refs)
    x_ref = next(it)
    if pre == "gated":
        y_ref, z_ref, ng_ref = next(it), next(it), next(it)
    elif pre:
        a_ref = next(it)
    if pre:
        pw_ref, pb_ref = next(it), next(it)
    g_ref, wg_ref, wu_ref, wd_ref = next(it), next(it), next(it), next(it)
    fg_ref = next(it) if final else None
    if n_post:
        pg_ref = next(it)
        w_refs = [next(it) for _ in range(n_post)]
        b_refs = [next(it) for _ in range(n_post)]
    o_ref = next(it)
    po_refs = [next(it) for _ in range(n_post)]
    hn_ref, acc_ref = next(it), next(it)
    xin_ref = next(it) if pre else x_ref
    if pre == "gated":
        xin = x_ref[...] + pb_ref[...]
        for c0 in range(0, D_INNER, GROUP_CH):
            cs = slice(c0, c0 + GROUP_CH)
            gated = y_ref[:, cs].astype(F32) * _silu(z_ref[:, cs].astype(F32))
            ms = jnp.mean(gated * gated, axis=-1, keepdims=True)
            a = (gated * lax.rsqrt(ms + EPS) * ng_ref[:, cs]).astype(BF16)
            xin = xin + _dot(a, pw_ref[cs, :])
        xin_ref[...] = xin
    elif pre:
        for c0 in range(0, D_MODEL, PROJ_CHUNK):
            cs = slice(c0, c0 + PROJ_CHUNK)
            xin_ref[:, cs] = x_ref[:, cs] + _dot(a_ref[...], pw_ref[:, cs]) + pb_ref[:, cs]
    hn_ref[...] = _rms(xin_ref[...], g_ref[...]).astype(BF16)
    for c in range(N_FF_CHUNKS):
        cs = slice(FF_CHUNK * c, FF_CHUNK * (c + 1))
        hn = hn_ref[...]
        gate = _dot(hn, wg_ref[:, cs])
        up = _dot(hn, wu_ref[:, cs])
        act = (_silu(gate) * up).astype(BF16)
        down = _dot(act, wd_ref[cs, :])
        if c == 0:
            acc_ref[...] = down
        else:
            acc_ref[...] += down
    out = xin_ref[...] + 0.5 * acc_ref[...]
    o_ref[...] = _rms(out, fg_ref[...]) if final else out
    if n_post:
        hn_ref[...] = _rms(out, pg_ref[...]).astype(BF16)
        _proj_body(hn_ref, w_refs, b_refs, po_refs, post_t)


def _ffn(x, g, ffn_w, layer, idx, *, pre=None, final_g=None, post=None, name="ffn"):
    m = x.shape[0]
    tm = min(TOKEN_TILE, m)
    tile = lambda n: pl.BlockSpec((tm, n), lambda i: (i, 0))
    w_spec = lambda r, c: pl.BlockSpec((None, None, r, c), lambda i: (layer, idx, 0, 0),
                                       pipeline_mode=pl.Buffered(1))
    args, in_specs = [x], [tile(D_MODEL)]
    scratch = [pltpu.VMEM((tm, D_MODEL), BF16), pltpu.VMEM((tm, D_MODEL), F32)]
    pre_kind = None
    if pre is not None:
        *acts, pw, pb = pre
        pre_kind = "gated" if len(acts) == 3 else "plain"
        if pre_kind == "gated":
            y, z, ng = acts
            args += [y, z, ng.reshape(1, D_INNER)]
            in_specs += [tile(D_INNER), tile(D_INNER), _const_spec((1, D_INNER))]
        else:
            args += acts
            in_specs += [tile(acts[0].shape[1])]
        args += [pw, pb.reshape(1, D_MODEL).astype(F32)]
        in_specs += [_const_spec(pw.shape), _const_spec((1, D_MODEL))]
        scratch.append(pltpu.VMEM((tm, D_MODEL), F32))
    args += [g.reshape(1, D_MODEL), *ffn_w]
    in_specs += [_const_spec((1, D_MODEL)), w_spec(D_MODEL, D_FF), w_spec(D_MODEL, D_FF), w_spec(D_FF, D_MODEL)]
    if final_g is not None:
        args.append(final_g.reshape(1, D_MODEL))
        in_specs.append(_const_spec((1, D_MODEL)))
    out_specs = [tile(D_MODEL)]
    out_shape = [jax.ShapeDtypeStruct((m, D_MODEL), F32)]
    n_post, post_t = 0, ()
    if post is not None:
        pg, ws, bs, dtypes, post_t = post
        n_post = len(ws)
        bs, w_specs, p_out_specs, p_out_shape = _proj_operands(ws, bs, dtypes, post_t, m, tm)
        args += [pg.reshape(1, D_MODEL), *ws, *bs]
        in_specs += [_const_spec((1, D_MODEL))] + w_specs
        out_specs += p_out_specs
        out_shape += p_out_shape
    return pl.pallas_call(
        functools.partial(_ffn_kernel, pre=pre_kind, final=final_g is not None,
                          n_post=n_post, post_t=tuple(post_t)),
        grid=(m // tm,),
        in_specs=in_specs,
        out_specs=out_specs,
        out_shape=out_shape,
        scratch_shapes=scratch,
        compiler_params=_params(1),
        name=name,
    )(*args)


CONV_PAD = 8


def _inproj_kernel(x_ref, g_ref, wz_ref, wx_ref, wdt_ref, conv0_ref, cw_ref, cb_ref, dtb_ref,
                   z_ref, xs_ref, bc_ref, dt_ref, convt_ref, hn_ref, stg_ref, carry_ref, *, tiles_per_seq):
    i = pl.program_id(0)
    tm = x_ref.shape[0]
    lo = CONV_PAD - (CONV_W - 1)
    hn_ref[...] = _rms(x_ref[...], g_ref[...]).astype(BF16)

    @pl.when(i % tiles_per_seq == 0)
    def _seq_start():
        carry_ref[lo:CONV_PAD, :] = conv0_ref[0]

    n_chunks = CONV_DIM // PROJ_CHUNK
    z_chunks = [slice(c0, c0 + PROJ_CHUNK) for c0 in range(0, D_INNER, PROJ_CHUNK)]
    xcs = lambda s: slice(PROJ_CHUNK * s, PROJ_CHUNK * (s + 1))
    r_next = _dot(hn_ref[...], wx_ref[:, xcs(0)])
    for s in range(n_chunks):
        cs, r, stg = xcs(s), r_next, stg_ref.at[s % 2]
        if s + 1 < n_chunks:
            r_next = _dot(hn_ref[...], wx_ref[:, xcs(s + 1)])
        if s < len(z_chunks):
            z_ref[:, z_chunks[s]] = _dot(hn_ref[...], wz_ref[:, z_chunks[s]]).astype(BF16)
        else:
            if s == len(z_chunks):
                dt_ref[...] = jax.nn.softplus(_dot(hn_ref[...], wdt_ref[...]) + dtb_ref[...])
        stg[lo:CONV_PAD, :] = carry_ref[lo:CONV_PAD, cs]
        stg[CONV_PAD:CONV_PAD + tm, :] = r
        acc = cb_ref[:, cs] + cw_ref[CONV_W - 1:CONV_W, cs] * r
        for k in range(CONV_W - 1):
            acc = acc + cw_ref[k:k + 1, cs] * stg[lo + k:lo + k + tm, :]
        carry_ref[lo:CONV_PAD, cs] = stg[lo + tm:CONV_PAD + tm, :]
        act = _silu(acc).astype(BF16)
        if PROJ_CHUNK * s < D_INNER:
            xs_ref[:, cs] = act
        else:
            bc_ref[:, PROJ_CHUNK * s - D_INNER:PROJ_CHUNK * (s + 1) - D_INNER] = act

    @pl.when(i % tiles_per_seq == tiles_per_seq - 1)
    def _seq_end():
        convt_ref[0] = carry_ref[lo:CONV_PAD, :]


def _inproj_conv(x, g, wz, wx, wdt, conv0, cw, cb, dtb, *, nseq):
    m = x.shape[0]
    tm = TOKEN_TILE
    tiles_per_seq = m // nseq // tm
    tile = lambda n: pl.BlockSpec((tm, n), lambda i: (i, 0))
    seq_conv = pl.BlockSpec((1, CONV_W - 1, CONV_DIM), lambda i: (i // tiles_per_seq, 0, 0))
    return pl.pallas_call(
        functools.partial(_inproj_kernel, tiles_per_seq=tiles_per_seq),
        grid=(m // tm,),
        in_specs=[tile(D_MODEL), _const_spec((1, D_MODEL)), _const_spec(wz.shape), _const_spec(wx.shape),
                  _const_spec(wdt.shape), seq_conv, _const_spec(cw.shape), _const_spec(cb.shape),
                  _const_spec(dtb.shape)],
        out_specs=[tile(D_INNER), tile(D_INNER), tile(2 * GROUP_CH), tile(LANES), seq_conv],
        out_shape=[jax.ShapeDtypeStruct((m, D_INNER), BF16), jax.ShapeDtypeStruct((m, D_INNER), BF16),
                   jax.ShapeDtypeStruct((m, 2 * GROUP_CH), BF16), jax.ShapeDtypeStruct((m, LANES), F32),
                   jax.ShapeDtypeStruct((nseq, CONV_W - 1, CONV_DIM), F32)],
        scratch_shapes=[pltpu.VMEM((tm, D_MODEL), BF16), pltpu.VMEM((2, CONV_PAD + tm, PROJ_CHUNK), F32),
                        pltpu.VMEM((CONV_PAD, CONV_DIM), F32)],
        compiler_params=_params(1),
        name="in_proj_conv",
    )(x, g.reshape(1, D_MODEL), wz, wx, wdt, conv0, cw, cb, dtb)


def _split3(v):
    hi = v.astype(BF16)
    r1 = v - hi.astype(F32)
    mid = r1.astype(BF16)
    lo = (r1 - mid.astype(F32)).astype(BF16)
    return hi, mid, lo


def _state_in(ssm0_ref, st_ref):
    for j in range(SSM_HEADS // 2):
        blk = ssm0_ref[0, 2 * j:2 * j + 2].reshape(LANES, D_STATE)
        st_ref[:, LANES * j:LANES * (j + 1)] = blk.T


def _state_out(st_ref, ssmt_ref):
    for j in range(SSM_HEADS // 2):
        blk = st_ref[:, LANES * j:LANES * (j + 1)].T
        ssmt_ref[0, 2 * j:2 * j + 2] = blk.reshape(2, SSM_HEAD_DIM, D_STATE)


def _decay_terms(dtv, alog_ref):
    L = CHUNK
    row_i = lax.broadcasted_iota(jnp.int32, (L, L), 0)
    col_i = lax.broadcasted_iota(jnp.int32, (L, L), 1)
    a = dtv * (-jnp.exp(alog_ref[...]))
    tri = jnp.where(col_i <= row_i, 1.0, 0.0).astype(BF16)
    acum_col = sum(_dot(tri, part) for part in _split3(a))
    acum_row = acum_col.T
    dt_row = dtv.T
    last = jnp.broadcast_to(acum_row[:, L - 1:L], (L, L))
    wd_row = jnp.exp(last - acum_row) * dt_row
    return acum_col, acum_row, dt_row, wd_row


def _ssd_chunk(xs_src, bc_src, dtv, alog_ref, dsk_ref, st_ref, y_ref):
    L = CHUNK
    row_i = lax.broadcasted_iota(jnp.int32, (L, L), 0)
    col_i = lax.broadcasted_iota(jnp.int32, (L, L), 1)
    causal = col_i <= row_i
    lane_lo = col_i < SSM_HEAD_DIM
    acum_col, acum_row, dt_row, wd_row = _decay_terms(dtv, alog_ref)

    for g in range(SSM_GROUPS):
        bg = bc_src[:, D_STATE * g:D_STATE * (g + 1)]
        cg = bc_src[:, GROUP_CH + D_STATE * g:GROUP_CH + D_STATE * (g + 1)]
        cb = lax.dot_general(cg, bg, NT_DIMS, preferred_element_type=F32)
        bgt = bg.astype(F32).T
        gsl = slice(GROUP_CH * g, GROUP_CH * (g + 1))
        y_inter = _dot(cg, st_ref[:, gsl].astype(BF16))
        outs = []
        for j in range(GROUP_CH // LANES):
            h0 = (GROUP_CH // SSM_HEAD_DIM) * g + 2 * j
            psl = slice(GROUP_CH * g + LANES * j, GROUP_CH * g + LANES * (j + 1))
            acs, lms, wms = [], [], []
            for h in (h0, h0 + 1):
                ac = jnp.broadcast_to(acum_col[:, h:h + 1], (L, L))
                seg = ac - acum_row[h:h + 1, :]
                lm = jnp.where(causal, jnp.exp(seg), 0.0) * cb * dt_row[h:h + 1, :]
                acs.append(ac)
                lms.append(lm.astype(BF16))
                wms.append((bgt * wd_row[h:h + 1, :]).astype(BF16))
            x = xs_src[:, psl]
            zero = jnp.zeros_like(x)
            xbd = jnp.concatenate([jnp.where(lane_lo, x, zero), jnp.where(lane_lo, zero, x)], axis=0)
            y_intra = _dot(jnp.concatenate(lms, axis=1), xbd)
            acp = jnp.where(lane_lo, acs[0], acs[1])
            y = y_intra + y_inter[:, LANES * j:LANES * (j + 1)] * jnp.exp(acp)
            outs.append(y + dsk_ref[:, psl] * x.astype(F32))
            upd = _dot(jnp.concatenate(wms, axis=1), xbd)
            st_ref[:, psl] = st_ref[:, psl] * jnp.exp(acp[L - 1:L, :]) + upd
        y_ref[0, :, gsl] = jnp.concatenate(outs, axis=1).astype(y_ref.dtype)


SHORT_ROWS = 16


def _ssd_chunk_short(xs_src, bc_src, dtv, alog_ref, dsk_ref, st_ref, y_ref, rows):
    L, R = CHUNK, SHORT_ROWS
    acum_col, acum_row, dt_row, wd_row = _decay_terms(dtv, alog_ref)
    row_i = lax.broadcasted_iota(jnp.int32, (R, L), 0)
    lane_i = lax.broadcasted_iota(jnp.int32, (R, L), 1)
    causal_p = ((lane_i & (R - 1)) <= row_i) & (lane_i < 2 * R)
    even_p = lane_i < R
    lane_lo = lane_i < SSM_HEAD_DIM
    lane1 = lax.broadcasted_iota(jnp.int32, (1, L), 1)

    def pair_row(rowform, h0):
        odd = pltpu.roll(rowform[h0 + 1:h0 + 2, :], R, axis=1)
        return jnp.where(lane1 < R, rowform[h0:h0 + 1, :], jnp.where(lane1 < 2 * R, odd, 0.0))

    def pair_tile(t):
        return t + pltpu.roll(t, R, axis=1)

    pad_rows = jnp.zeros((L - 2 * R, LANES), BF16)
    for g in range(SSM_GROUPS):
        bg = bc_src[:, D_STATE * g:D_STATE * (g + 1)]
        cg = bc_src[0:R, GROUP_CH + D_STATE * g:GROUP_CH + D_STATE * (g + 1)]
        cb2 = pair_tile(lax.dot_general(cg, bg, NT_DIMS, preferred_element_type=F32))
        bgt2 = pair_tile(bg.astype(F32).T)
        gsl = slice(GROUP_CH * g, GROUP_CH * (g + 1))
        y_inter = _dot(cg, st_ref[:, gsl].astype(BF16))
        outs = []
        for j in range(GROUP_CH // LANES):
            h0 = (GROUP_CH // SSM_HEAD_DIM) * g + 2 * j
            psl = slice(GROUP_CH * g + LANES * j, GROUP_CH * g + LANES * (j + 1))
            ac0 = jnp.broadcast_to(acum_col[0:R, h0:h0 + 1], (R, L))
            ac1 = jnp.broadcast_to(acum_col[0:R, h0 + 1:h0 + 2], (R, L))
            seg = jnp.where(even_p, ac0, ac1) - pair_row(acum_row, h0)
            lm = jnp.where(causal_p, jnp.exp(seg), 0.0) * cb2 * pair_row(dt_row, h0)
            x = xs_src[0:R, psl]
            zero = jnp.zeros_like(x)
            xbd = jnp.concatenate([jnp.where(lane_lo, x, zero), jnp.where(lane_lo, zero, x), pad_rows], axis=0)
            ach = jnp.where(lane_lo, ac0, ac1)
            y = _dot(lm.astype(BF16), xbd) + y_inter[:, LANES * j:LANES * (j + 1)] * jnp.exp(ach)
            outs.append(y + dsk_ref[:, psl] * x.astype(F32))
            upd = _dot((bgt2 * pair_row(wd_row, h0)).astype(BF16), xbd)
            st_ref[:, psl] = st_ref[:, psl] * jnp.exp(ach[R - 1:R, :]) + upd
        y_ref[0, :, gsl] = jnp.concatenate(outs, axis=1)[0:rows].astype(y_ref.dtype)


def _ssd_fused_kernel(xs_ref, bc_ref, dt_ref, ssm0_ref, alog_ref, dsk_ref, y_ref, ssmt_ref, st_ref, *, nchunks):
    c = pl.program_id(1)

    @pl.when(c == 0)
    def _init():
        _state_in(ssm0_ref, st_ref)

    _ssd_chunk(xs_ref.at[0], bc_ref.at[0], dt_ref[0], alog_ref, dsk_ref, st_ref, y_ref)

    @pl.when(c == nchunks - 1)
    def _fin():
        _state_out(st_ref, ssmt_ref)


def _ssd_fused(xs, bc, dt, ssm0, alog, dsk):
    nseq, ltot, _ = xs.shape
    nchunks = ltot // CHUNK
    seq_blk = lambda n: pl.BlockSpec((1, CHUNK, n), lambda s, c: (s, c, 0))
    state_blk = pl.BlockSpec((1, SSM_HEADS, SSM_HEAD_DIM, D_STATE), lambda s, c: (s, 0, 0, 0))
    return pl.pallas_call(
        functools.partial(_ssd_fused_kernel, nchunks=nchunks),
        grid=(nseq, nchunks),
        in_specs=[seq_blk(D_INNER), seq_blk(2 * GROUP_CH), seq_blk(LANES), state_blk,
                  _const_spec((1, LANES)), _const_spec((1, D_INNER))],
        out_specs=[seq_blk(D_INNER), state_blk],
        out_shape=[jax.ShapeDtypeStruct((nseq, ltot, D_INNER), BF16),
                   jax.ShapeDtypeStruct((nseq, SSM_HEADS, SSM_HEAD_DIM, D_STATE), F32)],
        scratch_shapes=[pltpu.VMEM((D_STATE, D_INNER), F32)],
        compiler_params=_params(2),
        name="ssd",
    )(xs, bc, dt, ssm0, alog, dsk)


def _ssd_step_kernel(xbc_ref, dt_ref, conv0_ref, ssm0_ref, cw_ref, cb_ref, dtb_ref, alog_ref,
                     dsk_ref, y_ref, convt_ref, ssmt_ref,
                     xpad_ref, xs_ref, bc_ref, st_ref, dtpad_ref, *, lin, lvalid):
    L, R = CHUNK, SHORT_ROWS
    lo = CONV_PAD - (CONV_W - 1)
    assert CONV_W - 1 <= lvalid <= lin <= R

    @pl.when(pl.program_id(0) == 0)
    def _zero():
        xpad_ref[...] = jnp.zeros_like(xpad_ref)
        xs_ref[...] = jnp.zeros_like(xs_ref)
        bc_ref[...] = jnp.zeros_like(bc_ref)
        dtpad_ref[...] = jnp.zeros_like(dtpad_ref)

    _state_in(ssm0_ref, st_ref)

    xpad_ref[lo:CONV_PAD, :] = conv0_ref[0]
    xpad_ref[CONV_PAD:CONV_PAD + lin, :] = xbc_ref[0].astype(F32)
    slab = GROUP_CH
    for s in range(CONV_DIM // slab):
        sl = slice(slab * s, slab * (s + 1))
        acc = cb_ref[:, sl] + cw_ref[CONV_W - 1:CONV_W, sl] * xpad_ref[CONV_PAD:CONV_PAD + R, sl]
        for k in range(CONV_W - 1):
            acc = acc + cw_ref[k:k + 1, sl] * xpad_ref[lo + k:lo + k + R, sl]
        act = _silu(acc).astype(BF16)
        if s < D_INNER // slab:
            xs_ref[0:R, sl] = act
        else:
            o = slab * s - D_INNER
            bc_ref[0:R, o:o + slab] = act
    convt_ref[0] = xpad_ref[CONV_PAD + lvalid - (CONV_W - 1):CONV_PAD + lvalid, :]

    dtpad_ref[0:lin, :] = dt_ref[0]
    row_i = lax.broadcasted_iota(jnp.int32, (R, LANES), 0)
    dtv = jnp.where(row_i < lvalid, jax.nn.softplus(dtpad_ref[...] + dtb_ref[...]), 0.0)
    dtv = jnp.concatenate([dtv, jnp.zeros((L - R, LANES), F32)], axis=0)
    _ssd_chunk_short(xs_ref, bc_ref, dtv, alog_ref, dsk_ref, st_ref, y_ref, lin)
    _state_out(st_ref, ssmt_ref)


def _ssd_step(xbc, dt, conv0, ssm0, cw, cb, dtb, alog, dsk, *, lvalid):
    nseq, lin, _ = xbc.shape
    seq_blk = lambda r, n: pl.BlockSpec((1, r, n), lambda s: (s, 0, 0))
    state_blk = pl.BlockSpec((1, SSM_HEADS, SSM_HEAD_DIM, D_STATE), lambda s: (s, 0, 0, 0))
    return pl.pallas_call(
        functools.partial(_ssd_step_kernel, lin=lin, lvalid=lvalid),
        grid=(nseq,),
        in_specs=[
            seq_blk(lin, CONV_DIM), seq_blk(lin, LANES),
            seq_blk(CONV_W - 1, CONV_DIM), state_blk,
            _const_spec((CONV_W, CONV_DIM)), _const_spec((1, CONV_DIM)),
            _const_spec((1, LANES)), _const_spec((1, LANES)), _const_spec((1, D_INNER)),
        ],
        out_specs=[seq_blk(lin, D_INNER), seq_blk(CONV_W - 1, CONV_DIM), state_blk],
        out_shape=[
            jax.ShapeDtypeStruct((nseq, lin, D_INNER), BF16),
            jax.ShapeDtypeStruct((nseq, CONV_W - 1, CONV_DIM), F32),
            jax.ShapeDtypeStruct((nseq, SSM_HEADS, SSM_HEAD_DIM, D_STATE), F32),
        ],
        scratch_shapes=[
            pltpu.VMEM((2 * CONV_PAD + SHORT_ROWS, CONV_DIM), F32),
            pltpu.VMEM((CHUNK, D_INNER), BF16),
            pltpu.VMEM((CHUNK, 2 * GROUP_CH), BF16),
            pltpu.VMEM((D_STATE, D_INNER), F32),
            pltpu.VMEM((SHORT_ROWS, LANES), F32),
        ],
        compiler_params=_params(1),
        name="ssd_step",
    )(xbc, dt, conv0, ssm0, cw, cb, dtb, alog, dsk)


def _attn_core(qt, kc, kp, vt, sink_ref, prev_offset, qw):
    W = WINDOW
    GW = Q_PER_KV * qw
    jj = lax.broadcasted_iota(jnp.int32, (W, GW), 0)
    tt = lax.broadcasted_iota(jnp.int32, (W, GW), 1) & (qw - 1)
    cur = jj <= tt
    prev = jj > tt + prev_offset
    lane_row = lax.broadcasted_iota(jnp.int32, (1, GW), 1)
    scores = []
    for g in range(N_KV_HEADS):
        gs = slice(HEAD_DIM * g, HEAD_DIM * (g + 1))
        qg = jnp.concatenate([qt[HEAD_DIM * (Q_PER_KV * g + r):HEAD_DIM * (Q_PER_KV * g + r + 1), 0:qw]
                              for r in range(Q_PER_KV)], axis=1)
        qg = qg.astype(BF16) * (HEAD_DIM ** -0.5)
        scores.append((_dot(kc[:, gs], qg), _dot(kp[:, gs], qg)))
    outs = []
    for g in range(N_KV_HEADS):
        gs = slice(HEAD_DIM * g, HEAD_DIM * (g + 1))
        s_c, s_p = scores[g]
        s = jnp.where(cur, s_c, jnp.where(prev, s_p, -jnp.inf))
        sink = jnp.full((1, GW), sink_ref[Q_PER_KV * g], F32)
        for r in range(1, Q_PER_KV):
            sink = jnp.where(lane_row >= r * qw, sink_ref[Q_PER_KV * g + r], sink)
        m = jnp.maximum(jnp.max(s, axis=0, keepdims=True), sink)
        p = jnp.exp(s - m)
        denom = jnp.sum(p, axis=0, keepdims=True) + jnp.exp(sink - m)
        zero = jnp.zeros_like(p)
        pcat = jnp.concatenate([jnp.where(cur, p, zero).astype(BF16),
                                jnp.where(cur, zero, p).astype(BF16)], axis=0)
        ot = _dot(vt[gs, :], pcat) * (1.0 / denom)
        outs.extend(ot[:, r * qw:(r + 1) * qw] for r in range(Q_PER_KV))
    ot_all = jnp.concatenate(outs, axis=0)
    if qw < W:
        ot_all = jnp.concatenate([ot_all, jnp.zeros((D_MODEL, W - qw), F32)], axis=1)
    return ot_all.T


def _attn_blocks_kernel(qt_ref, kp_ref, vtp_ref, kc_ref, vtc_ref, sink_ref, o_ref):
    i = pl.program_id(1)
    vt = jnp.concatenate([vtc_ref[...], vtp_ref[...]], axis=1).astype(BF16)
    o = _attn_core(qt_ref[...], kc_ref[0].astype(BF16), kp_ref[0].astype(BF16), vt, sink_ref,
                   jnp.where(i > 0, 0, WINDOW), WINDOW)
    o_ref[0] = o.astype(o_ref.dtype)


def _attn_blocks(qt, k, vt, sinks, *, nseq):
    ltot = k.shape[1]
    nblk = ltot // WINDOW
    cur3 = lambda s, i: (s, i, 0)
    prev3 = lambda s, i: (s, jnp.maximum(i - 1, 0), 0)
    cur_t = lambda s, i: (0, s * nblk + i)
    prev_t = lambda s, i: (0, s * nblk + jnp.maximum(i - 1, 0))
    return pl.pallas_call(
        _attn_blocks_kernel,
        grid=(nseq, nblk),
        in_specs=[
            pl.BlockSpec((D_MODEL, WINDOW), cur_t),
            pl.BlockSpec((1, WINDOW, KV_DIM), prev3),
            pl.BlockSpec((KV_DIM, WINDOW), prev_t),
            pl.BlockSpec((1, WINDOW, KV_DIM), cur3),
            pl.BlockSpec((KV_DIM, WINDOW), cur_t),
            pl.BlockSpec(memory_space=pltpu.SMEM),
        ],
        out_specs=pl.BlockSpec((1, WINDOW, D_MODEL), cur3),
        out_shape=jax.ShapeDtypeStruct((nseq, ltot, D_MODEL), BF16),
        compiler_params=_params(2),
        name="attn",
    )(qt, k, vt, k, vt, sinks)


STEP_QUERY_LANES = LANES // Q_PER_KV


def _attn_step_kernel(q_ref, ck_ref, cv_ref, kn_ref, vn_ref, sink_ref, o_ref, kw_ref, vw_ref,
                      qpad_ref, kcpad_ref, vcpad_ref, *, tq, valid):
    W = WINDOW
    assert tq <= STEP_QUERY_LANES

    @pl.when(pl.program_id(0) == 0)
    def _zero():
        qpad_ref[...] = jnp.zeros_like(qpad_ref)
        kcpad_ref[...] = jnp.zeros_like(kcpad_ref)
        vcpad_ref[...] = jnp.zeros_like(vcpad_ref)

    qpad_ref[0:tq, :] = q_ref[0].astype(F32)
    kcpad_ref[0:tq, :] = kn_ref[0]
    vcpad_ref[0:tq, :] = vn_ref[0]
    kp = jnp.concatenate([ck_ref[0, :, g, :] for g in range(N_KV_HEADS)], axis=1)
    vp = jnp.concatenate([cv_ref[0, :, g, :] for g in range(N_KV_HEADS)], axis=1)
    vt = jnp.concatenate([vcpad_ref[...].T, vp.T], axis=1).astype(BF16)
    o = _attn_core(qpad_ref[...].T, kcpad_ref[...].astype(BF16), kp.astype(BF16), vt, sink_ref, 0,
                   STEP_QUERY_LANES)
    o_ref[0] = o[0:tq].astype(o_ref.dtype)

    kw_ref[0, 0:W - valid] = ck_ref[0, valid:W]
    vw_ref[0, 0:W - valid] = cv_ref[0, valid:W]
    for g in range(N_KV_HEADS):
        gs = slice(HEAD_DIM * g, HEAD_DIM * (g + 1))
        kw_ref[0, W - valid:W, g, :] = kn_ref[0, 0:valid, gs]
        vw_ref[0, W - valid:W, g, :] = vn_ref[0, 0:valid, gs]


def _attn_step(q, cache_k, cache_v, k_new, v_new, sinks, *, valid):
    nseq, tq, _ = q.shape
    seq3 = lambda r, n: pl.BlockSpec((1, r, n), lambda s: (s, 0, 0))
    win = pl.BlockSpec((1, WINDOW, N_KV_HEADS, HEAD_DIM), lambda s: (s, 0, 0, 0))
    win_shape = jax.ShapeDtypeStruct((nseq, WINDOW, N_KV_HEADS, HEAD_DIM), F32)
    return pl.pallas_call(
        functools.partial(_attn_step_kernel, tq=tq, valid=valid),
        grid=(nseq,),
        in_specs=[seq3(tq, D_MODEL), win, win, seq3(tq, KV_DIM), seq3(tq, KV_DIM),
                  pl.BlockSpec(memory_space=pltpu.SMEM)],
        out_specs=[seq3(tq, D_MODEL), win, win],
        out_shape=[jax.ShapeDtypeStruct((nseq, tq, D_MODEL), BF16), win_shape, win_shape],
        scratch_shapes=[pltpu.VMEM((WINDOW, D_MODEL), F32), pltpu.VMEM((WINDOW, KV_DIM), F32),
                        pltpu.VMEM((WINDOW, KV_DIM), F32)],
        compiler_params=_params(1),
        name="attn_step",
    )(q, cache_k, cache_v, k_new, v_new, sinks)


def _trunk_prompt(x3, p):
    nseq, ltot, _ = x3.shape
    m = nseq * ltot
    g, ffn_w = p["norm_gain"], p["ffn"]
    x = x3.reshape(m, D_MODEL)
    (x,) = _ffn(x, g[0, 0], ffn_w, 0, 0)
    conv0 = jnp.zeros((nseq, CONV_W - 1, CONV_DIM), F32)
    ssm0 = jnp.zeros((nseq, SSM_HEADS, SSM_HEAD_DIM, D_STATE), F32)
    z, xs, bc, dt, conv_t = _inproj_conv(x, g[0, 1], p["w_z"], p["w_xbc"], p["w_dt"], conv0,
                                         p["conv_w"], p["conv_b"], p["dt_bias"], nseq=nseq)
    seq = lambda t: t.reshape(nseq, ltot, t.shape[-1])
    y, ssm_t = _ssd_fused(seq(xs), seq(bc), seq(dt), ssm0, p["a_log"], p["d_skip"])
    x, k, vt = _ffn(x, g[0, 2], ffn_w, 0, 1,
                    pre=(y.reshape(m, D_INNER), z, p["ssm_norm"], p["w_out"], jnp.zeros((D_MODEL,), F32)),
                    post=(p["kv_norm"], [p["w_k"], p["w_v_t"]], [p["b_k"], p["b_v"]], [F32, F32], (False, True)),
                    name="ffn_outproj_kv")
    x, qt = _ffn(x, g[1, 0], ffn_w, 1, 0, post=(g[1, 1], [p["w_q_t"]], [p["b_q"]], [BF16], (True,)),
                 name="ffn_q")
    k3 = k.reshape(nseq, ltot, KV_DIM)
    o = _attn_blocks(qt, k3, vt, p["sinks"], nseq=nseq)
    (y_out,) = _ffn(x, g[1, 2], ffn_w, 1, 1, pre=(o.reshape(m, D_MODEL), p["w_o"], p["b_o"]),
                    final_g=p["final_norm"], name="ffn_oproj_final")
    win = lambda t: t.reshape(nseq, WINDOW, N_KV_HEADS, HEAD_DIM)
    v_tail = jnp.stack([vt[:, (s + 1) * ltot - WINDOW:(s + 1) * ltot].T for s in range(nseq)])
    return y_out.reshape(nseq, ltot, D_MODEL), conv_t, ssm_t, win(k3[:, ltot - WINDOW:]), win(v_tail)


def _trunk_step(x3, conv0, ssm0, cache_k, cache_v, p, *, valid):
    nseq, lin, _ = x3.shape
    m = nseq * lin
    g, ffn_w = p["norm_gain"], p["ffn"]
    x = x3.reshape(m, D_MODEL)
    (x,) = _ffn(x, g[0, 0], ffn_w, 0, 0)
    z, xbc, dt = _proj(x, g[0, 1], [p["w_z"], p["w_xbc"], p["w_dt"]],
                       [jnp.zeros((D_INNER,), F32), jnp.zeros((CONV_DIM,), F32), jnp.zeros((LANES,), F32)],
                       [BF16, BF16, F32], "in_proj")
    seq = lambda t: t.reshape(nseq, lin, t.shape[-1])
    y, conv_t, ssm_t = _ssd_step(seq(xbc), seq(dt), conv0, ssm0, p["conv_w"], p["conv_b"],
                                 p["dt_bias"], p["a_log"], p["d_skip"], lvalid=valid)
    x, k, v = _ffn(x, g[0, 2], ffn_w, 0, 1,
                   pre=(y.reshape(m, D_INNER), z, p["ssm_norm"], p["w_out"], jnp.zeros((D_MODEL,), F32)),
                   post=(p["kv_norm"], [p["w_k"], p["w_v"]], [p["b_k"], p["b_v"]], [F32, F32], (False, False)),
                   name="ffn_outproj_kv")
    x, q = _ffn(x, g[1, 0], ffn_w, 1, 0, post=(g[1, 1], [p["w_q"]], [p["b_q"]], [BF16], (False,)),
                name="ffn_q")
    o, k_win, v_win = _attn_step(seq(q), cache_k, cache_v, seq(k), seq(v), p["sinks"], valid=valid)
    (y_out,) = _ffn(x, g[1, 2], ffn_w, 1, 1, pre=(o.reshape(m, D_MODEL), p["w_o"], p["b_o"]),
                    final_g=p["final_norm"], name="ffn_oproj_final")
    return y_out.reshape(nseq, lin, D_MODEL), conv_t, ssm_t, k_win, v_win


def kernel(x_prompt, x_sample, state_conv, state_ssm, cache_k_win, cache_v_win, norm_gain, ffn_w_gate, ffn_w_up, ffn_w_down, ssm_w_in, ssm_conv_w, ssm_conv_b, ssm_dt_bias, ssm_a_log, ssm_d, ssm_norm, ssm_w_out, kv_norm, attn_w_kv, attn_b_kv, attn_w_q, attn_b_q, attn_sinks, attn_w_o, attn_b_o, final_norm):
    assert ffn_w_gate.shape[0] == 2 and ssm_w_in.shape[0] == 1 and attn_w_q.shape[0] == 1
    w_in = ssm_w_in[0].astype(BF16)
    pad_heads = lambda v: jnp.pad(v.astype(F32), (0, LANES - SSM_HEADS)).reshape(1, LANES)
    w_kv = attn_w_kv.astype(BF16)
    w_q = attn_w_q[0].astype(BF16)
    p = {
        "norm_gain": norm_gain,
        "ffn": (ffn_w_gate.astype(BF16), ffn_w_up.astype(BF16), ffn_w_down.astype(BF16)),
        "w_z": w_in[:, :D_INNER],
        "w_xbc": w_in[:, D_INNER:D_INNER + CONV_DIM],
        "w_dt": jnp.pad(w_in[:, D_INNER + CONV_DIM:], ((0, 0), (0, LANES - SSM_HEADS))),
        "conv_w": ssm_conv_w[0].astype(F32),
        "conv_b": ssm_conv_b[0].astype(F32).reshape(1, CONV_DIM),
        "dt_bias": pad_heads(ssm_dt_bias[0]),
        "a_log": pad_heads(ssm_a_log[0]),
        "d_skip": jnp.repeat(ssm_d[0].astype(F32), SSM_HEAD_DIM).reshape(1, D_INNER),
        "ssm_norm": ssm_norm[0].astype(F32).reshape(1, D_INNER),
        "w_out": ssm_w_out[0].astype(BF16),
        "kv_norm": kv_norm,
        "w_k": w_kv[:, :KV_DIM], "w_v": w_kv[:, KV_DIM:], "w_v_t": w_kv[:, KV_DIM:].T,
        "b_k": attn_b_kv[:KV_DIM], "b_v": attn_b_kv[KV_DIM:],
        "w_q": w_q, "w_q_t": w_q.T,
        "b_q": attn_b_q[0],
        "sinks": attn_sinks[0].astype(F32),
        "w_o": attn_w_o[0].astype(BF16),
        "b_o": attn_b_o[0],
        "final_norm": final_norm,
    }

    y_p, conv_p, ssm_p, k_win_p, v_win_p = _trunk_prompt(x_prompt, p)

    dbsz, dseq, _ = x_sample.shape
    lin_s = 8
    assert CONV_W - 1 <= dseq <= lin_s
    xs = jnp.pad(x_sample, ((0, 0), (0, lin_s - dseq), (0, 0)))
    y_s, conv_s, ssm_s, k_win_s, v_win_s = _trunk_step(xs, state_conv[0], state_ssm[0], cache_k_win,
                                                       cache_v_win, p, valid=dseq)
    return (y_p, y_s[:, :dseq], conv_p[None], ssm_p[None], k_win_p, v_win_p,
            conv_s[None], ssm_s[None], k_win_s, v_win_s)
```
